```python
import math, functools
import jax, jax.numpy as jnp
from jax import lax
import numpy as np

D_MODEL = 1024
BATCH = 4
SEQ = 4096
DEPTH = 4
DEC_BATCH = 32
DEC_SEQ = 32
PAST_LEN = 4096

CHUNK = 64
Q_BLOCK = 128
N_HEADS = 8
HEAD_DIM = 64
ATT_V_DIM = N_HEADS * 2 * HEAD_DIM
ROPE_THETA = 10000.0
SSM_EXPAND = 2
D_INNER = SSM_EXPAND * D_MODEL
SSM_HEAD_DIM = 64
SSM_HEADS = D_INNER // SSM_HEAD_DIM
SSM_GROUPS = 8
HEADS_PER_GROUP = SSM_HEADS // SSM_GROUPS
D_STATE = 128
CONV_W = 4
CONV_DIM = D_INNER + 2 * SSM_GROUPS * D_STATE
SSD_CHUNK = CHUNK
Q_DIM = N_HEADS * 2 * HEAD_DIM
K_DIM = N_HEADS * 2 * HEAD_DIM
V_DIM = ATT_V_DIM
GATE_DIM = 2 * D_MODEL
IN_DIM = Q_DIM + K_DIM + V_DIM + D_INNER + CONV_DIM + SSM_HEADS + GATE_DIM
PK_HEADS = 8
N_KEYS = 128
N_EXPERTS = N_KEYS * N_KEYS
PK_TOPK = 16
PK_DIM = 256
PK_HALF = PK_DIM // 2
PEER_BLOCK = 256
EPS = 1e-6

kernel_name = "diffattn_ssd_peer_streaming_step"


def rmsnorm(x, g):
    xf = x.astype(jnp.float32)
    y = xf * lax.rsqrt(jnp.mean(xf * xf, axis=-1, keepdims=True) + EPS)
    return (y * g.astype(jnp.float32)).astype(x.dtype)


def rope(x, pos):
    half = HEAD_DIM // 2
    inv_freq = ROPE_THETA ** (-jnp.arange(half, dtype=jnp.float32) / half)
    ang = pos[:, None] * inv_freq[None, :]
    cos = jnp.cos(ang)[None, :, None, None, :]
    sin = jnp.sin(ang)[None, :, None, None, :]
    x1 = x[..., :half].astype(jnp.float32)
    x2 = x[..., half:].astype(jnp.float32)
    return jnp.concatenate([x1 * cos - x2 * sin, x2 * cos + x1 * sin], axis=-1).astype(x.dtype)


def lambda_init(layer):
    return 0.8 - 0.6 * math.exp(-0.3 * layer)


def diff_attn_prompt(q, k, v, lam):
    bsz, L = q.shape[:2]
    nb = L // Q_BLOCK
    qb = q.reshape(bsz, nb, Q_BLOCK, N_HEADS, 2, HEAD_DIM).transpose(1, 0, 2, 3, 4, 5)
    key_chunk = jnp.arange(L) // CHUNK

    def one_block(args):
        qblk, blk = args
        s = jnp.einsum('bqhcd,bkhcd->bhcqk', qblk, k).astype(jnp.float32)
        q_chunk = (blk * Q_BLOCK + jnp.arange(Q_BLOCK)) // CHUNK
        mask = key_chunk[None, :] <= q_chunk[:, None]
        s = jnp.where(mask, s, -jnp.inf)
        prob = jax.nn.softmax(s, axis=-1)
        w = (prob[:, :, 0] - lam * prob[:, :, 1]).astype(v.dtype)
        return jnp.einsum('bhqk,bkhe->bqhe', w, v)

    out = lax.map(one_block, (qb, jnp.arange(nb)))
    return out.transpose(1, 0, 2, 3, 4).reshape(bsz, L, N_HEADS, 2 * HEAD_DIM)


def diff_attn_sample(q, k, v, k_past, v_past, lam):
    n_past = k_past.shape[1]
    s = jnp.concatenate([jnp.einsum('bqhcd,bkhcd->bhcqk', q, k_past),
                         jnp.einsum('bqhcd,bkhcd->bhcqk', q, k)], axis=-1).astype(jnp.float32)
    prob = jax.nn.softmax(s, axis=-1)
    w = (prob[:, :, 0] - lam * prob[:, :, 1]).astype(v.dtype)
    return (jnp.einsum('bhqk,bkhe->bqhe', w[..., :n_past], v_past)
            + jnp.einsum('bhqk,bkhe->bqhe', w[..., n_past:], v))


def segsum(t):
    T = t.shape[-1]
    cs = jnp.cumsum(t, axis=-1)
    diff = cs[..., :, None] - cs[..., None, :]
    mask = jnp.tril(jnp.ones((T, T), dtype=bool))
    return jnp.where(mask, diff, -jnp.inf)


def ssd_scan(x, dt, a, b_mat, c_mat, init_state):
    bsz, L = x.shape[:2]
    pad = (-L) % SSD_CHUNK
    padf = lambda t: jnp.pad(t, [(0, 0), (0, pad)] + [(0, 0)] * (t.ndim - 2))
    x, dt, b_mat, c_mat = padf(x), padf(dt), padf(b_mat), padf(c_mat)
    nc = (L + pad) // SSD_CHUNK
    xdt = (x * dt[..., None]).reshape(bsz, nc, SSD_CHUNK, SSM_GROUPS, HEADS_PER_GROUP, SSM_HEAD_DIM)
    da = (dt.astype(jnp.float32) * a.astype(jnp.float32)).reshape(bsz, nc, SSD_CHUNK, SSM_GROUPS, HEADS_PER_GROUP)
    da = da.transpose(0, 3, 4, 1, 2)
    bm = b_mat.reshape(bsz, nc, SSD_CHUNK, SSM_GROUPS, D_STATE)
    cm = c_mat.reshape(bsz, nc, SSD_CHUNK, SSM_GROUPS, D_STATE)
    a_cum = jnp.cumsum(da, axis=-1)
    decay_in = jnp.exp(segsum(da))
    cb = jnp.einsum('bclgn,bcsgn->bgcls', cm, bm)
    y_diag = jnp.einsum('bghcls,bcsghp->bclghp', cb[:, :, None] * decay_in, xdt)
    decay_states = jnp.exp(a_cum[..., -1:] - a_cum).transpose(0, 3, 4, 1, 2)
    states = jnp.einsum('bclgn,bclghp->bcghpn', bm, xdt * decay_states[..., None])
    init = init_state.reshape(bsz, SSM_GROUPS, HEADS_PER_GROUP, SSM_HEAD_DIM, D_STATE).astype(states.dtype)
    states = jnp.concatenate([init[:, None], states], axis=1)
    chunk_decay = jnp.pad(a_cum[..., -1], [(0, 0), (0, 0), (0, 0), (1, 0)])
    decay_chunk = jnp.exp(segsum(chunk_decay))
    new_states = jnp.einsum('bghzc,bcghpn->bzghpn', decay_chunk, states)
    prev_states, final_state = new_states[:, :-1], new_states[:, -1]
    decay_out = jnp.exp(a_cum).transpose(0, 3, 4, 1, 2)
    y_off = jnp.einsum('bclgn,bcghpn->bclghp', cm, prev_states) * decay_out[..., None]
    y = (y_diag + y_off).reshape(bsz, nc * SSD_CHUNK, SSM_HEADS, SSM_HEAD_DIM)[:, :L]
    return (y.astype(x.dtype),
            final_state.reshape(bsz, SSM_HEADS, SSM_HEAD_DIM, D_STATE).astype(init_state.dtype))


def peer(h, wq, k1, k2, u, v):
    bsz, L, _ = h.shape
    t = h.reshape(-1, D_MODEL)
    n_tok = t.shape[0]
    pad = (-n_tok) % PEER_BLOCK
    t = jnp.pad(t, ((0, pad), (0, 0))).reshape(-1, PEER_BLOCK, D_MODEL)

    def one_block(xb):
        q = (xb @ wq).reshape(PEER_BLOCK, PK_HEADS, 2, PK_HALF)
        s1 = jnp.einsum('thd,hnd->thn', q[:, :, 0], k1).astype(jnp.float32)
        s2 = jnp.einsum('thd,hnd->thn', q[:, :, 1], k2).astype(jnp.float32)
        v1, i1 = lax.top_k(s1, PK_TOPK)
        v2, i2 = lax.top_k(s2, PK_TOPK)
        cand = (v1[..., :, None] + v2[..., None, :]).reshape(PEER_BLOCK, PK_HEADS, PK_TOPK * PK_TOPK)
        sv, ci = lax.top_k(cand, PK_TOPK)
        idx = (jnp.take_along_axis(i1, ci // PK_TOPK, axis=-1) * N_KEYS
               + jnp.take_along_axis(i2, ci % PK_TOPK, axis=-1))
        g = jax.nn.softmax(sv, axis=-1)
        act = jax.nn.gelu(jnp.einsum('td,thkd->thk', xb, u[idx]), approximate=False)
        coef = (act.astype(jnp.float32) * g).astype(xb.dtype)
        return jnp.einsum('thk,thkd->td', coef, v[idx])

    out = lax.map(one_block, t).reshape(-1, D_MODEL)[:n_tok]
    return out.reshape(bsz, L, D_MODEL)


def trunk_layer(x, pos, conv_prev, ssm_prev, attend, lam_init, p):
    bsz, L, _ = x.shape
    h = rmsnorm(x, p['norm_mix'])
    proj = h @ p['w_in']
    offs = [int(o) for o in np.cumsum([Q_DIM, K_DIM, V_DIM, D_INNER, CONV_DIM, SSM_HEADS])]
    q, k, v, z, xbc, dt, gates = jnp.split(proj, offs, axis=-1)
    q = q.reshape(bsz, L, N_HEADS, 2, HEAD_DIM)
    k = k.reshape(bsz, L, N_HEADS, 2, HEAD_DIM)
    v = v.reshape(bsz, L, N_HEADS, 2 * HEAD_DIM)
    q = rope(rmsnorm(q, p['q_norm']), pos) * (HEAD_DIM ** -0.5)
    k = rope(rmsnorm(k, p['k_norm']), pos)
    lam = (jnp.exp(jnp.sum(p['lam_q1'].astype(jnp.float32) * p['lam_k1'].astype(jnp.float32)))
           - jnp.exp(jnp.sum(p['lam_q2'].astype(jnp.float32) * p['lam_k2'].astype(jnp.float32)))
           + lam_init)
    att = attend(q, k, v, lam)
    att = (rmsnorm(att, p['attn_subln']) * (1.0 - lam_init)).reshape(bsz, L, ATT_V_DIM)
    xpad = jnp.concatenate([conv_prev.astype(xbc.dtype), xbc], axis=1)
    conv = sum(xpad[:, j:j + L] * p['conv_w'][j] for j in range(CONV_W)) + p['conv_b']
    conv = jax.nn.silu(conv)
    xs, bm, cm = jnp.split(conv, [D_INNER, D_INNER + SSM_GROUPS * D_STATE], axis=-1)
    xs = xs.reshape(bsz, L, SSM_HEADS, SSM_HEAD_DIM)
    dt = jax.nn.softplus(dt + p['dt_bias'])
    a = -jnp.exp(p['a_log'])
    y, ssm_new = ssd_scan(xs, dt, a, bm.reshape(bsz, L, SSM_GROUPS, D_STATE),
                          cm.reshape(bsz, L, SSM_GROUPS, D_STATE), ssm_prev)
    y = y + p['d_skip'][:, None] * xs
    y = (y.reshape(bsz, L, D_INNER) * jax.nn.silu(z)).reshape(bsz, L, SSM_GROUPS, D_INNER // SSM_GROUPS)
    y = rmsnorm(y, p['ssm_norm'].reshape(SSM_GROUPS, D_INNER // SSM_GROUPS)).reshape(bsz, L, D_INNER)
    g_att, g_ssm = jnp.split(jax.nn.sigmoid(gates), 2, axis=-1)
    merged = g_att * (att @ p['w_proj_attn']) + g_ssm * (y @ p['w_proj_ssm'])
    x = x + merged @ p['w_out']
    x = x + peer(rmsnorm(x, p['norm_ffn']), p['peer_wq'], p['peer_k1'], p['peer_k2'], p['peer_u'], p['peer_v'])
    return x, k, v, ssm_new, xpad[:, -(CONV_W - 1):]


def setup_inputs(seed: int = 0) -> dict:
    key = jax.random.key(seed)
    ks = jax.random.split(key, 32)
    f32 = jnp.float32

    def nrm(k, shape, scale):
        return jax.random.normal(k, shape, f32) * scale

    def gain(k, shape):
        return 1.0 + 0.02 * jax.random.normal(k, shape, f32)

    dt0 = jnp.exp(jax.random.uniform(ks[17], (DEPTH, SSM_HEADS), f32, math.log(1e-3), math.log(1e-1)))
    dt_bias = dt0 + jnp.log(-jnp.expm1(-dt0))
    a_log = jnp.log(jax.random.uniform(ks[18], (DEPTH, SSM_HEADS), f32, 1.0, 16.0))
    return {
        'x_prompt': nrm(ks[0], (BATCH, SEQ, D_MODEL), 1.0),
        'x_sample': nrm(ks[1], (DEC_BATCH, DEC_SEQ, D_MODEL), 1.0),
        'cache_k': nrm(ks[2], (DEPTH, DEC_BATCH, PAST_LEN, N_HEADS, 2, HEAD_DIM), 1.0),
        'cache_v': nrm(ks[3], (DEPTH, DEC_BATCH, PAST_LEN, N_HEADS, 2 * HEAD_DIM), 1.0),
        'state_ssm': nrm(ks[4], (DEPTH, DEC_BATCH, SSM_HEADS, SSM_HEAD_DIM, D_STATE), 0.5),
        'state_conv': nrm(ks[5], (DEPTH, DEC_BATCH, CONV_W - 1, CONV_DIM), 1.0),
        'norm_mix': gain(ks[6], (DEPTH, D_MODEL)),
        'w_in': nrm(ks[7], (DEPTH, D_MODEL, IN_DIM), D_MODEL ** -0.5),
        'q_norm': gain(ks[8], (DEPTH, HEAD_DIM)),
        'k_norm': gain(ks[9], (DEPTH, HEAD_DIM)),
        'lam_q1': nrm(ks[10], (DEPTH, HEAD_DIM), 0.1),
        'lam_k1': nrm(ks[11], (DEPTH, HEAD_DIM), 0.1),
        'lam_q2': nrm(ks[12], (DEPTH, HEAD_DIM), 0.1),
        'lam_k2': nrm(ks[13], (DEPTH, HEAD_DIM), 0.1),
        'attn_subln': gain(ks[14], (DEPTH, 2 * HEAD_DIM)),
        'conv_w': nrm(ks[15], (DEPTH, CONV_W, CONV_DIM), CONV_W ** -0.5),
        'conv_b': nrm(ks[16], (DEPTH, CONV_DIM), 0.02),
        'dt_bias': dt_bias,
        'a_log': a_log,
        'd_skip': gain(ks[19], (DEPTH, SSM_HEADS)),
        'ssm_norm': gain(ks[20], (DEPTH, D_INNER)),
        'w_proj_attn': nrm(ks[21], (DEPTH, ATT_V_DIM, D_MODEL), ATT_V_DIM ** -0.5),
        'w_proj_ssm': nrm(ks[22], (DEPTH, D_INNER, D_MODEL), D_INNER ** -0.5),
        'w_out': nrm(ks[23], (DEPTH, D_MODEL, D_MODEL), D_MODEL ** -0.5),
        'norm_ffn': gain(ks[24], (DEPTH, D_MODEL)),
        'peer_wq': nrm(ks[25], (DEPTH, D_MODEL, PK_HEADS * PK_DIM), D_MODEL ** -0.5),
        'peer_k1': nrm(ks[26], (DEPTH, PK_HEADS, N_KEYS, PK_HALF), PK_HALF ** -0.5),
        'peer_k2': nrm(ks[27], (DEPTH, PK_HEADS, N_KEYS, PK_HALF), PK_HALF ** -0.5),
        'peer_u': nrm(ks[28], (DEPTH, N_EXPERTS, D_MODEL), D_MODEL ** -0.5),
        'peer_v': nrm(ks[29], (DEPTH, N_EXPERTS, D_MODEL), D_MODEL ** -0.5),
    }


def reference(x_prompt, x_sample, cache_k, cache_v, state_ssm, state_conv, norm_mix, w_in, q_norm, k_norm,
              lam_q1, lam_k1, lam_q2, lam_k2, attn_subln, conv_w, conv_b, dt_bias, a_log, d_skip, ssm_norm,
              w_proj_attn, w_proj_ssm, w_out, norm_ffn, peer_wq, peer_k1, peer_k2, peer_u, peer_v):
    bp, lp = x_prompt.shape[:2]
    ls = x_sample.shape[1]
    n_past = cache_k.shape[2]
    pos_p = jnp.arange(lp, dtype=jnp.float32)
    pos_s = n_past + jnp.arange(ls, dtype=jnp.float32)
    conv0_p = jnp.zeros((bp, CONV_W - 1, CONV_DIM), x_prompt.dtype)
    ssm0_p = jnp.zeros((bp, SSM_HEADS, SSM_HEAD_DIM, D_STATE), x_prompt.dtype)
    hp, hs = x_prompt, x_sample
    kp_l, vp_l, sp_l, cp_l, ks_l, vs_l, ss_l, cs_l = [], [], [], [], [], [], [], []
    for i in range(DEPTH):
        p = {'norm_mix': norm_mix[i], 'w_in': w_in[i], 'q_norm': q_norm[i], 'k_norm': k_norm[i],
             'lam_q1': lam_q1[i], 'lam_k1': lam_k1[i], 'lam_q2': lam_q2[i], 'lam_k2': lam_k2[i],
             'attn_subln': attn_subln[i], 'conv_w': conv_w[i], 'conv_b': conv_b[i], 'dt_bias': dt_bias[i],
             'a_log': a_log[i], 'd_skip': d_skip[i], 'ssm_norm': ssm_norm[i], 'w_proj_attn': w_proj_attn[i],
             'w_proj_ssm': w_proj_ssm[i], 'w_out': w_out[i], 'norm_ffn': norm_ffn[i], 'peer_wq': peer_wq[i],
             'peer_k1': peer_k1[i], 'peer_k2': peer_k2[i], 'peer_u': peer_u[i], 'peer_v': peer_v[i]}
        li = lambda_init(i)
        hp, kp, vp, sp, cp = trunk_layer(hp, pos_p, conv0_p, ssm0_p, diff_attn_prompt, li, p)
        attend_s = functools.partial(diff_attn_sample, k_past=cache_k[i], v_past=cache_v[i])
        hs, ks_, vs_, ss_, cs_ = trunk_layer(hs, pos_s, state_conv[i], state_ssm[i],
                                             lambda q, k, v, lam, f=attend_s: f(q, k, v, lam=lam), li, p)
        kp_l.append(kp); vp_l.append(vp); sp_l.append(sp); cp_l.append(cp)
        ks_l.append(ks_); vs_l.append(vs_); ss_l.append(ss_); cs_l.append(cs_)
    k_prompt = jnp.stack(kp_l); v_prompt = jnp.stack(vp_l)
    ssm_prompt = jnp.stack(sp_l); conv_prompt = jnp.stack(cp_l)
    k_sample = jnp.stack(ks_l); v_sample = jnp.stack(vs_l)
    ssm_sample = jnp.stack(ss_l); conv_sample = jnp.stack(cs_l)
    return (hp, hs, k_prompt, v_prompt, ssm_prompt, conv_prompt, k_sample, v_sample, ssm_sample, conv_sample)
```

```python
import functools
import math

import jax
import jax.numpy as jnp
from jax import lax
from jax.experimental import pallas as pl
from jax.experimental.pallas import tpu as pltpu

F32 = jnp.float32
BF16 = jnp.bfloat16

D_MODEL = 1024
DEPTH = 4
CHUNK = 64
N_HEADS = 8
HEAD_DIM = 64
ATT_DIM = N_HEADS * 2 * HEAD_DIM
ROPE_THETA = 10000.0
D_INNER = 2048
SSM_HEAD_DIM = 64
SSM_HEADS = D_INNER // SSM_HEAD_DIM
SSM_GROUPS = 8
HEADS_PER_GROUP = SSM_HEADS // SSM_GROUPS
D_STATE = 128
CONV_W = 4
CONV_DIM = D_INNER + 2 * SSM_GROUPS * D_STATE
GATE_DIM = 2 * D_MODEL
PK_HEADS = 8
N_KEYS = 128
N_EXPERTS = N_KEYS * N_KEYS
PK_TOPK = 16
PK_HALF = 128
EPS = 1e-6

LANES = 128
SUBLANES = 8
VMEM_LIMIT = 56 * 1024 * 1024

COL_Z = 0
COL_GATE = 2048
COL_XBC = 4096
COL_Q = 8192
COL_K = 9216
COL_V = 10240
COL_DT = 11264
PROJ_COLS = 11520
PROJ_TN = 1152
ATT_TQ = 512
ATT_TK = 512
ATT_TKS = 512

HIGHEST = lax.Precision.HIGHEST
NT_DIMS = (((1,), (1,)), ((), ()))


def _params(sem):
    return pltpu.CompilerParams(dimension_semantics=sem, vmem_limit_bytes=VMEM_LIMIT)


def _dot(a, b):
    return jnp.dot(a, b, preferred_element_type=F32)


def _dot_nt(a, b):
    return lax.dot_general(a, b, NT_DIMS, preferred_element_type=F32)


def _dot_exact(a, b):
    return jnp.dot(a, b, preferred_element_type=F32, precision=HIGHEST)


def _transpose_exact(x):
    c = x.shape[1]
    eye = (lax.broadcasted_iota(jnp.int32, (c, c), 0) == lax.broadcasted_iota(jnp.int32, (c, c), 1)).astype(F32)
    return lax.dot_general(eye, x, NT_DIMS, preferred_element_type=F32, precision=HIGHEST)


def _sigmoid(x):
    return 1.0 / (1.0 + jnp.exp(-x))


def _silu(x):
    return x * _sigmoid(x)


def _softplus(x):
    return jnp.maximum(x, 0.0) + jnp.log(1.0 + jnp.exp(-jnp.abs(x)))


def _in_proj_kernel(x_ref, g_ref, w_ref, o_ref, h_scr):
    @pl.when(pl.program_id(1) == 0)
    def _():
        x = x_ref[...]
        ms = jnp.mean(x * x, axis=-1, keepdims=True)
        h_scr[...] = (x * lax.rsqrt(ms + EPS) * g_ref[...]).astype(BF16)

    o_ref[...] = _dot(h_scr[...], w_ref[...])


def in_proj(x, gain, w, *, tm):
    t = x.shape[0]
    n = w.shape[1]
    return pl.pallas_call(
        _in_proj_kernel,
        out_shape=jax.ShapeDtypeStruct((t, n), F32),
        grid=(t // tm, n // PROJ_TN),
        in_specs=[pl.BlockSpec((tm, D_MODEL), lambda i, j: (i, 0)),
                  pl.BlockSpec((1, D_MODEL), lambda i, j: (0, 0)),
                  pl.BlockSpec((D_MODEL, PROJ_TN), lambda i, j: (0, j))],
        out_specs=pl.BlockSpec((tm, PROJ_TN), lambda i, j: (i, j)),
        scratch_shapes=[pltpu.VMEM((tm, D_MODEL), BF16)],
        compiler_params=_params(("parallel", "arbitrary")),
        name="in_proj",
    )(x, gain, w)


def _qk_prep_kernel(q_ref, k_ref, v_ref, cos_ref, sin_ref, qg_ref, kg_ref,
                    qb_ref, ko_ref, kb_ref, vo_ref, vb_ref):
    cos = cos_ref[...]
    sin = sin_ref[...]
    row = lax.broadcasted_iota(jnp.int32, (LANES, LANES), 0) // HEAD_DIM
    col = lax.broadcasted_iota(jnp.int32, (LANES, LANES), 1) // HEAD_DIM
    group_mean = jnp.where(row == col, 1.0 / HEAD_DIM, 0.0).astype(F32)
    lane = lax.broadcasted_iota(jnp.int32, (1, LANES), 1)
    low_half = (lane % HEAD_DIM) < (HEAD_DIM // 2)

    def norm_rope(x, gain):
        ms = _dot_exact(x * x, group_mean)
        y = x * lax.rsqrt(ms + EPS) * gain
        partner = jnp.where(low_half, pltpu.roll(y, LANES - HEAD_DIM // 2, 1), pltpu.roll(y, HEAD_DIM // 2, 1))
        return y * cos + partner * sin

    for h in range(N_HEADS):
        sl = slice(h * LANES, (h + 1) * LANES)
        q = norm_rope(q_ref[:, sl], qg_ref[...]) * (HEAD_DIM ** -0.5)
        qb_ref[:, sl] = q.astype(BF16)
        k = norm_rope(k_ref[:, sl], kg_ref[...])
        ko_ref[:, sl] = k
        kb_ref[:, sl] = k.astype(BF16)
    v = v_ref[...]
    vo_ref[...] = v
    vb_ref[...] = v.astype(BF16)


def qk_prep(proj, cos, sin, q_gain, k_gain, *, tm):
    t = proj.shape[0]
    blk = lambda c: pl.BlockSpec((tm, ATT_DIM), lambda i, c=c: (i, c // ATT_DIM))
    tab = pl.BlockSpec((tm, LANES), lambda i: (i, 0))
    vec = pl.BlockSpec((1, LANES), lambda i: (0, 0))
    out = pl.BlockSpec((tm, ATT_DIM), lambda i: (i, 0))
    return pl.pallas_call(
        _qk_prep_kernel,
        out_shape=[jax.ShapeDtypeStruct((t, ATT_DIM), BF16),
                   jax.ShapeDtypeStruct((t, ATT_DIM), F32),
                   jax.ShapeDtypeStruct((t, ATT_DIM), BF16),
                   jax.ShapeDtypeStruct((t, ATT_DIM), F32),
                   jax.ShapeDtypeStruct((t, ATT_DIM), BF16)],
        grid=(t // tm,),
        in_specs=[blk(COL_Q), blk(COL_K), blk(COL_V), tab, tab, vec, vec],
        out_specs=[out, out, out, out, out],
        compiler_params=_params(("parallel",)),
        name="qk_prep",
    )(proj, proj, proj, cos, sin, q_gain, k_gain)


def _lambda_full(lam_ref, lam_init):
    a = jnp.sum(lam_ref[0:1, :] * lam_ref[1:2, :], axis=-1, keepdims=True)
    b = jnp.sum(lam_ref[2:3, :] * lam_ref[3:4, :], axis=-1, keepdims=True)
    return jnp.exp(a) - jnp.exp(b) + lam_init


def _split_components(q):
    lane = lax.broadcasted_iota(jnp.int32, (1, LANES), 1)
    zero = jnp.zeros_like(q)
    return jnp.concatenate([jnp.where(lane < HEAD_DIM, q, zero), jnp.where(lane >= HEAD_DIM, q, zero)], axis=0)


def _flash_update(s, v, m_ref, l_ref, acc_ref):
    m_old = m_ref[...]
    m_new = jnp.maximum(m_old, jnp.max(s, axis=-1, keepdims=True))
    alpha = jnp.exp(m_old - m_new)
    p = jnp.exp(s - m_new)
    l_ref[...] = alpha * l_ref[...] + jnp.sum(p, axis=-1, keepdims=True)
    acc_ref[...] = alpha * acc_ref[...] + _dot(p.astype(BF16), v)
    m_ref[...] = m_new


def _diff_combine(acc, l, rows, lam, gain, lam_init):
    o1 = acc[:rows] / l[:rows]
    o2 = acc[rows:] / l[rows:]
    o = o1 - lam * o2
    ms = jnp.mean(o * o, axis=-1, keepdims=True)
    return o * lax.rsqrt(ms + EPS) * gain * (1.0 - lam_init)


def _attn_prompt_kernel(lam_ref, g_ref, q_ref, k_ref, v_ref, o_ref, m_scr, l_scr, acc_scr, *, tq, tk, lam_init):
    i = pl.program_id(2)
    qq = _split_components(q_ref[...])
    m_scr[...] = jnp.full(m_scr.shape, -jnp.inf, F32)
    l_scr[...] = jnp.zeros(l_scr.shape, F32)
    acc_scr[...] = jnp.zeros(acc_scr.shape, F32)

    def scores(j):
        start = pl.multiple_of(j * tk, tk)
        return _dot_nt(qq, k_ref[pl.ds(start, tk), :]), v_ref[pl.ds(start, tk), :]

    diag = (i * tq) // tk

    def full_block(j, carry):
        s, v = scores(j)
        _flash_update(s, v, m_scr, l_scr, acc_scr)
        return carry

    lax.fori_loop(0, diag, full_block, 0)

    s, v = scores(diag)
    q_pos = i * tq + lax.broadcasted_iota(jnp.int32, (2 * tq, tk), 0) % tq
    k_pos = diag * tk + lax.broadcasted_iota(jnp.int32, (2 * tq, tk), 1)
    s = jnp.where(k_pos // CHUNK <= q_pos // CHUNK, s, -jnp.inf)
    _flash_update(s, v, m_scr, l_scr, acc_scr)

    lam = _lambda_full(lam_ref, lam_init)
    o_ref[...] = _diff_combine(acc_scr[...], l_scr[...], tq, lam, g_ref[...], lam_init).astype(o_ref.dtype)


def attn_prompt(qb, kb, vb, lam_rows, sub_gain, lam_init, *, batch, seq, tq, tk):
    nq = seq // tq
    assert tk % tq == 0 and tq % CHUNK == 0, "one key block must hold all chunks of a query block"
    kern = functools.partial(_attn_prompt_kernel, tq=tq, tk=tk, lam_init=lam_init)
    return pl.pallas_call(
        kern,
        out_shape=jax.ShapeDtypeStruct((batch * seq, ATT_DIM), BF16),
        grid=(batch, N_HEADS, nq),
        in_specs=[pl.BlockSpec((4, HEAD_DIM), lambda b, h, i: (0, 0)),
                  pl.BlockSpec((1, LANES), lambda b, h, i: (0, 0)),
                  pl.BlockSpec((tq, LANES), lambda b, h, i: (b * nq + i, h)),
                  pl.BlockSpec((seq, LANES), lambda b, h, i: (b, h)),
                  pl.BlockSpec((seq, LANES), lambda b, h, i: (b, h))],
        out_specs=pl.BlockSpec((tq, LANES), lambda b, h, i: (b * nq + i, h)),
        scratch_shapes=[pltpu.VMEM((2 * tq, 1), F32), pltpu.VMEM((2 * tq, 1), F32),
                        pltpu.VMEM((2 * tq, LANES), F32)],
        compiler_params=_params(("parallel", "parallel", "arbitrary")),
        name="attn_prompt",
    )(lam_rows, sub_gain, qb, kb, vb)


def _attn_sample_kernel(lam_ref, g_ref, q_ref, kn_ref, vn_ref, kc_ref, vc_ref, o_ref,
                        m_scr, l_scr, acc_scr, *, rows, lam_init):
    j = pl.program_id(1)

    @pl.when(j == 0)
    def _():
        m_scr[...] = jnp.full(m_scr.shape, -jnp.inf, F32)
        l_scr[...] = jnp.zeros(l_scr.shape, F32)
        acc_scr[...] = jnp.zeros(acc_scr.shape, F32)

    def update(k_all, v_all):
        for h in range(N_HEADS):
            sl = slice(h * LANES, (h + 1) * LANES)
            qq = _split_components(q_ref[:, sl])
            s = _dot_nt(qq, k_all[:, sl].astype(BF16))
            _flash_update(s, v_all[:, sl].astype(BF16), m_scr.at[h], l_scr.at[h], acc_scr.at[h])

    update(kc_ref[...], vc_ref[...])

    @pl.when(j == pl.num_programs(1) - 1)
    def _():
        update(kn_ref[...], vn_ref[...])
        lam = _lambda_full(lam_ref, lam_init)
        for h in range(N_HEADS):
            sl = slice(h * LANES, (h + 1) * LANES)
            o_ref[:, sl] = _diff_combine(acc_scr[h], l_scr[h], rows, lam, g_ref[...], lam_init).astype(o_ref.dtype)


def attn_sample(qb, kb, vb, cache_k, cache_v, lam_rows, sub_gain, lam_init, *, layer, batch, rows, row0, tk):
    past = cache_k.shape[2]
    new_blk = pl.BlockSpec((rows, ATT_DIM), lambda b, j: (row0 // rows + b, 0))
    cache_blk = pl.BlockSpec((None, None, tk, ATT_DIM), lambda b, j: (layer, b, j, 0))
    kern = functools.partial(_attn_sample_kernel, rows=rows, lam_init=lam_init)
    return pl.pallas_call(
        kern,
        out_shape=jax.ShapeDtypeStruct((batch * rows, ATT_DIM), BF16),
        grid=(batch, past // tk),
        in_specs=[pl.BlockSpec((4, HEAD_DIM), lambda b, j: (0, 0)),
                  pl.BlockSpec((1, LANES), lambda b, j: (0, 0)),
                  new_blk, new_blk, new_blk, cache_blk, cache_blk],
        out_specs=pl.BlockSpec((rows, ATT_DIM), lambda b, j: (b, 0)),
        scratch_shapes=[pltpu.VMEM((N_HEADS, 2 * rows, 1), F32), pltpu.VMEM((N_HEADS, 2 * rows, 1), F32),
                        pltpu.VMEM((N_HEADS, 2 * rows, LANES), F32)],
        compiler_params=_params(("parallel", "arbitrary")),
        name="attn_sample",
    )(lam_rows, sub_gain, qb, kb, vb, cache_k, cache_v)


def _ssd_kernel(xbc_ref, z_ref, dt_ref, conv0_ref, ssm0_ref, cw_ref, cb_ref, dtb_ref, alog_ref, dskip_ref, ng_ref,
                y_ref, ssm_ref, conv_ref, xpad_scr, *, lc):
    c = pl.program_id(1)
    pad0 = SUBLANES - (CONV_W - 1)

    @pl.when(c == 0)
    def _():
        xpad_scr[pad0:SUBLANES, :] = conv0_ref[...]
        ssm_ref[...] = ssm0_ref[...]

    xpad_scr[SUBLANES:, :] = xbc_ref[...]
    conv = cb_ref[...] + sum(xpad_scr[pad0 + j:pad0 + j + lc, :] * cw_ref[j:j + 1, :] for j in range(CONV_W))
    conv = _silu(conv)
    tail = xpad_scr[lc + pad0:lc + SUBLANES, :]
    xpad_scr[pad0:SUBLANES, :] = tail
    conv_ref[...] = tail

    dt = _softplus(dt_ref[...] + dtb_ref[...])
    da = dt * (-jnp.exp(alog_ref[...]))
    row = lax.broadcasted_iota(jnp.int32, (lc, lc), 0)
    col = lax.broadcasted_iota(jnp.int32, (lc, lc), 1)
    causal = row >= col
    a_cum = _dot_exact(causal.astype(F32), da)
    dt_t = _transpose_exact(dt)
    a_cum_t = _transpose_exact(a_cum)
    eye = (lax.broadcasted_iota(jnp.int32, (LANES, LANES), 0)
           == lax.broadcasted_iota(jnp.int32, (LANES, LANES), 1)).astype(BF16)

    for g in range(SSM_GROUPS):
        b_g = conv[:, D_INNER + g * D_STATE:D_INNER + (g + 1) * D_STATE].astype(BF16)
        c_off = D_INNER + SSM_GROUPS * D_STATE
        c_g = conv[:, c_off + g * D_STATE:c_off + (g + 1) * D_STATE].astype(BF16)
        cb = _dot_nt(c_g, b_g)
        ys = []
        for pair in range(HEADS_PER_GROUP // 2):
            lane0 = (g * HEADS_PER_GROUP + 2 * pair) * SSM_HEAD_DIM
            x_pair = conv[:, lane0:lane0 + LANES].astype(BF16)
            x_pair_t = _dot_nt(eye, x_pair)
            for sub in range(2):
                h = g * HEADS_PER_GROUP + 2 * pair + sub
                x_h = x_pair[:, sub * SSM_HEAD_DIM:(sub + 1) * SSM_HEAD_DIM]
                x_h_t = x_pair_t[sub * SSM_HEAD_DIM:(sub + 1) * SSM_HEAD_DIM, :]
                a_col = a_cum[:, h:h + 1]
                a_row = a_cum_t[h:h + 1, :]
                dt_row = dt_t[h:h + 1, :]
                a_last = a_cum_t[h:h + 1, lc - 1:lc]
                decay = jnp.exp(jnp.where(causal, a_col - a_row, -jnp.inf))
                w_in = (cb * decay * dt_row).astype(BF16)
                state = ssm_ref[h]
                y_h = _dot(w_in, x_h) + jnp.exp(a_col) * _dot_nt(c_g, state.astype(BF16))
                ys.append(y_h)
                w_state = (x_h_t * (dt_row * jnp.exp(a_last - a_row))).astype(BF16)
                ssm_ref[h] = jnp.exp(a_last) * state + _dot(w_state, b_g)
        lanes = slice(g * HEADS_PER_GROUP * SSM_HEAD_DIM, (g + 1) * HEADS_PER_GROUP * SSM_HEAD_DIM)
        y_g = jnp.concatenate(ys, axis=-1) + dskip_ref[:, lanes] * conv[:, lanes]
        y_g = y_g * _silu(z_ref[:, lanes])
        ms = jnp.mean(y_g * y_g, axis=-1, keepdims=True)
        y_ref[:, lanes] = (y_g * lax.rsqrt(ms + EPS) * ng_ref[:, lanes]).astype(y_ref.dtype)


def ssd(proj, conv0, ssm0, conv_w, conv_b, dt_bias, a_log, d_skip, norm_gain, *, batch, seq, row0, lc):
    nc = seq // lc
    tok = lambda width, col: pl.BlockSpec((lc, width), lambda b, c: (row0 // lc + b * nc + c, col // width))
    const = lambda r, w: pl.BlockSpec((r, w), lambda b, c: (0, 0))
    kern = functools.partial(_ssd_kernel, lc=lc)
    return pl.pallas_call(
        kern,
        out_shape=[jax.ShapeDtypeStruct((batch * seq, D_INNER), BF16),
                   jax.ShapeDtypeStruct((batch, SSM_HEADS, SSM_HEAD_DIM, D_STATE), F32),
                   jax.ShapeDtypeStruct((batch, CONV_W - 1, CONV_DIM), F32)],
        grid=(batch, nc),
        in_specs=[tok(CONV_DIM, COL_XBC), tok(D_INNER, COL_Z), tok(LANES, COL_DT),
                  pl.BlockSpec((None, CONV_W - 1, CONV_DIM), lambda b, c: (b, 0, 0)),
                  pl.BlockSpec((None, SSM_HEADS, SSM_HEAD_DIM, D_STATE), lambda b, c: (b, 0, 0, 0)),
                  const(CONV_W, CONV_DIM), const(1, CONV_DIM), const(1, LANES), const(1, LANES),
                  const(1, D_INNER), const(1, D_INNER)],
        out_specs=[pl.BlockSpec((lc, D_INNER), lambda b, c: (b * nc + c, 0)),
                   pl.BlockSpec((None, SSM_HEADS, SSM_HEAD_DIM, D_STATE), lambda b, c: (b, 0, 0, 0)),
                   pl.BlockSpec((None, CONV_W - 1, CONV_DIM), lambda b, c: (b, 0, 0))],
        scratch_shapes=[pltpu.VMEM((SUBLANES + lc, CONV_DIM), F32)],
        compiler_params=_params(("parallel", "arbitrary")),
        name="ssd",
    )(proj, proj, proj, conv0, ssm0, conv_w, conv_b, dt_bias, a_log, d_skip, norm_gain)


def _out_proj_kernel(x_ref, att_ref, y_ref, ga_ref, gm_ref, wa_ref, wm_ref, wo_ref, o_ref):
    merged = (_sigmoid(ga_ref[...]) * _dot(att_ref[...], wa_ref[...])
              + _sigmoid(gm_ref[...]) * _dot(y_ref[...], wm_ref[...]))
    o_ref[...] = x_ref[...] + _dot(merged.astype(BF16), wo_ref[...])


def out_proj(x, att, y, proj, wa, wm, wo, *, tm):
    t = x.shape[0]
    gate = lambda c: pl.BlockSpec((tm, D_MODEL), lambda i, c=c: (i, c // D_MODEL))
    const = lambda r: pl.BlockSpec((r, D_MODEL), lambda i: (0, 0))
    return pl.pallas_call(
        _out_proj_kernel,
        out_shape=jax.ShapeDtypeStruct((t, D_MODEL), F32),
        grid=(t // tm,),
        in_specs=[pl.BlockSpec((tm, D_MODEL), lambda i: (i, 0)),
                  pl.BlockSpec((tm, ATT_DIM), lambda i: (i, 0)),
                  pl.BlockSpec((tm, D_INNER), lambda i: (i, 0)),
                  gate(COL_GATE), gate(COL_GATE + D_MODEL),
                  const(ATT_DIM), const(D_INNER), const(D_MODEL)],
        out_specs=pl.BlockSpec((tm, D_MODEL), lambda i: (i, 0)),
        compiler_params=_params(("parallel",)),
        name="out_proj",
    )(x, att, y, proj, proj, wa, wm, wo)


def _top_values(s, k):
    out = []
    for _ in range(k):
        m = jnp.max(s, axis=0, keepdims=True)
        out.append(m)
        s = jnp.where(s == m, -jnp.inf, s)
    return out


def _peer_keys_kernel(x_ref, g_ref, wqt_ref, k1_ref, k2_ref, h_ref, s1_ref, s2_ref, st_ref):
    x = x_ref[...]
    ms = jnp.mean(x * x, axis=-1, keepdims=True)
    h = (x * lax.rsqrt(ms + EPS) * g_ref[...]).astype(BF16)
    h_ref[...] = h
    q_t = _dot_nt(wqt_ref[...], h)
    for hd in range(PK_HEADS):
        r0 = hd * 2 * PK_HALF
        s1 = _dot(k1_ref[hd], q_t[r0:r0 + PK_HALF, :].astype(BF16))
        s2 = _dot(k2_ref[hd], q_t[r0 + PK_HALF:r0 + 2 * PK_HALF, :].astype(BF16))
        s1_ref[hd] = s1
        s2_ref[hd] = s2
        v1 = _top_values(s1, PK_TOPK)
        v2 = jnp.concatenate(_top_values(s2, PK_TOPK), axis=0)
        cand = jnp.concatenate([a + v2 for a in v1], axis=0)
        tau = _top_values(cand, PK_TOPK)[-1]
        top = v1[0] + v2[0:1, :]
        z = jnp.sum(jnp.where(cand >= tau, jnp.exp(cand - top), 0.0), axis=0, keepdims=True)
        st_ref[hd] = jnp.concatenate([tau, top, 1.0 / z, jnp.zeros((SUBLANES - 3, tau.shape[1]), F32)], axis=0)


def peer_keys(x, gain, wq_t, k1, k2, *, tt):
    t = x.shape[0]
    per_head = lambda r: pl.BlockSpec((PK_HEADS, r, tt), lambda i: (0, 0, i))
    return pl.pallas_call(
        _peer_keys_kernel,
        out_shape=[jax.ShapeDtypeStruct((t, D_MODEL), BF16),
                   jax.ShapeDtypeStruct((PK_HEADS, N_KEYS, t), F32),
                   jax.ShapeDtypeStruct((PK_HEADS, N_KEYS, t), F32),
                   jax.ShapeDtypeStruct((PK_HEADS, SUBLANES, t), F32)],
        grid=(t // tt,),
        in_specs=[pl.BlockSpec((tt, D_MODEL), lambda i: (i, 0)),
                  pl.BlockSpec((1, D_MODEL), lambda i: (0, 0)),
                  pl.BlockSpec((PK_HEADS * 2 * PK_HALF, D_MODEL), lambda i: (0, 0)),
                  pl.BlockSpec((PK_HEADS, N_KEYS, PK_HALF), lambda i: (0, 0, 0)),
                  pl.BlockSpec((PK_HEADS, N_KEYS, PK_HALF), lambda i: (0, 0, 0))],
        out_specs=[pl.BlockSpec((tt, D_MODEL), lambda i: (i, 0)),
                   per_head(N_KEYS), per_head(N_KEYS), per_head(SUBLANES)],
        compiler_params=_params(("parallel",)),
        name="peer_keys",
    )(x, gain, wq_t, k1, k2)


def _gelu(x):
    return 0.5 * x * (1.0 + lax.erf(x * (2.0 ** -0.5)))


def _peer_mix_kernel(x_ref, h_ref, u_ref, vt_ref, s1_ref, s2_ref, st_ref, o_ref, acc_scr, *, eb):
    e = pl.program_id(1)

    @pl.when(e == 0)
    def _():
        acc_scr[...] = jnp.zeros(acc_scr.shape, F32)

    act = _gelu(_dot_nt(u_ref[...], h_ref[...]))
    rows_per_step = eb // N_KEYS
    gates = []
    for r in range(rows_per_step):
        gate = jnp.zeros((N_KEYS, act.shape[1]), F32)
        for hd in range(PK_HEADS):
            i1 = e * rows_per_step + r
            s = s1_ref[hd, pl.ds(i1, 1), :] + s2_ref[hd]
            tau = st_ref[hd, 0:1, :]
            top = st_ref[hd, 1:2, :]
            inv_z = st_ref[hd, 2:3, :]
            gate = gate + jnp.where(s >= tau, jnp.exp(s - top) * inv_z, 0.0)
        gates.append(gate)
    coef = (act * jnp.concatenate(gates, axis=0)).astype(BF16)
    acc_scr[...] += _dot(vt_ref[...], coef)

    @pl.when(e == pl.num_programs(1) - 1)
    def _():
        o_ref[...] = x_ref[...] + acc_scr[...].T


def peer_mix(x, h, u, v_t, s1, s2, stats, *, tt, eb):
    t = x.shape[0]
    per_head = lambda r: pl.BlockSpec((PK_HEADS, r, tt), lambda i, e: (0, 0, i))
    kern = functools.partial(_peer_mix_kernel, eb=eb)
    return pl.pallas_call(
        kern,
        out_shape=jax.ShapeDtypeStruct((t, D_MODEL), F32),
        grid=(t // tt, N_EXPERTS // eb),
        in_specs=[pl.BlockSpec((tt, D_MODEL), lambda i, e: (i, 0)),
                  pl.BlockSpec((tt, D_MODEL), lambda i, e: (i, 0)),
                  pl.BlockSpec((eb, D_MODEL), lambda i, e: (e, 0)),
                  pl.BlockSpec((D_MODEL, eb), lambda i, e: (0, e)),
                  per_head(N_KEYS), per_head(N_KEYS), per_head(SUBLANES)],
        out_specs=pl.BlockSpec((tt, D_MODEL), lambda i, e: (i, 0)),
        scratch_shapes=[pltpu.VMEM((D_MODEL, tt), F32)],
        compiler_params=_params(("parallel", "arbitrary")),
        name="peer_mix",
    )(x, h, u, v_t, s1, s2, stats)


def _rope_tables(pos):
    half = HEAD_DIM // 2
    inv_freq = ROPE_THETA ** (-jnp.arange(half, dtype=F32) / half)
    ang = pos[:, None] * inv_freq[None, :]
    cos, sin = jnp.cos(ang), jnp.sin(ang)
    reps = LANES // HEAD_DIM
    return jnp.tile(jnp.concatenate([cos, cos], -1), (1, reps)), jnp.tile(jnp.concatenate([-sin, sin], -1), (1, reps))


def _lambda_init(layer):
    return 0.8 - 0.6 * math.exp(-0.3 * layer)


def _pad_lanes(v, width=LANES):
    return jnp.pad(v, (0, width - v.shape[0])).reshape(1, width)


def kernel(x_prompt, x_sample, cache_k, cache_v, state_ssm, state_conv, norm_mix, w_in, q_norm, k_norm, lam_q1, lam_k1, lam_q2, lam_k2, attn_subln, conv_w, conv_b, dt_bias, a_log, d_skip, ssm_norm, w_proj_attn, w_proj_ssm, w_out, norm_ffn, peer_wq, peer_k1, peer_k2, peer_u, peer_v):
    bp, lp, _ = x_prompt.shape
    bs, ls, _ = x_sample.shape
    depth = w_in.shape[0]
    n_past = cache_k.shape[2]
    tp, ts = bp * lp, bs * ls
    x = jnp.concatenate([x_prompt.reshape(tp, D_MODEL), x_sample.reshape(ts, D_MODEL)], axis=0)
    t_all = tp + ts
    tm = math.gcd(t_all, 1024)
    tt = math.gcd(t_all, 256)

    pos = jnp.concatenate([jnp.tile(jnp.arange(lp, dtype=F32), bp),
                           jnp.tile(n_past + jnp.arange(ls, dtype=F32), bs)])
    cos, sin = _rope_tables(pos)
    cache_k = cache_k.reshape(depth, bs, n_past, ATT_DIM)
    cache_v = cache_v.reshape(depth, bs, n_past, ATT_DIM)
    conv0_p = jnp.zeros((bp, CONV_W - 1, CONV_DIM), F32)
    ssm0_p = jnp.zeros((bp, SSM_HEADS, SSM_HEAD_DIM, D_STATE), F32)

    outs = {name: [] for name in ("kp", "vp", "sp", "cp", "ks", "vs", "ss", "cs")}
    for i in range(depth):
        li = _lambda_init(i)
        q_w, k_w, v_w, z_w, xbc_w, dt_w, gate_w = jnp.split(
            w_in[i], [int(o) for o in (1024, 2048, 3072, 5120, 9216, 9248)], axis=-1)
        w_all = jnp.concatenate([z_w, gate_w, xbc_w, q_w, k_w, v_w, dt_w,
                                 jnp.zeros((D_MODEL, PROJ_COLS - COL_DT - SSM_HEADS), F32)], axis=-1).astype(BF16)
        proj = in_proj(x, norm_mix[i].reshape(1, D_MODEL), w_all, tm=tm)

        q_gain = jnp.tile(q_norm[i], LANES // HEAD_DIM).reshape(1, LANES)
        k_gain = jnp.tile(k_norm[i], LANES // HEAD_DIM).reshape(1, LANES)
        qb, k_rot, kb, v_f32, vb = qk_prep(proj, cos, sin, q_gain, k_gain, tm=min(tm, 256))

        lam_rows = jnp.stack([lam_q1[i], lam_k1[i], lam_q2[i], lam_k2[i]])
        sub_gain = attn_subln[i].reshape(1, LANES)
        att_p = attn_prompt(qb, kb, vb, lam_rows, sub_gain, li, batch=bp, seq=lp,
                            tq=min(ATT_TQ, lp), tk=min(ATT_TK, lp))
        att_s = attn_sample(qb, kb, vb, cache_k, cache_v, lam_rows, sub_gain, li,
                            layer=i, batch=bs, rows=ls, row0=tp, tk=min(ATT_TKS, n_past))
        att = jnp.concatenate([att_p, att_s], axis=0)

        ssd_w = (conv_w[i], conv_b[i].reshape(1, CONV_DIM), _pad_lanes(dt_bias[i]), _pad_lanes(a_log[i]),
                 jnp.repeat(d_skip[i], SSM_HEAD_DIM).reshape(1, D_INNER), ssm_norm[i].reshape(1, D_INNER))
        y_p, ssm_p, conv_p = ssd(proj, conv0_p, ssm0_p, *ssd_w, batch=bp, seq=lp, row0=0, lc=min(CHUNK, lp))
        y_s, ssm_s, conv_s = ssd(proj, state_conv[i], state_ssm[i], *ssd_w, batch=bs, seq=ls, row0=tp,
                                 lc=min(CHUNK, ls))
        y = jnp.concatenate([y_p, y_s], axis=0)

        x = out_proj(x, att, y, proj, w_proj_attn[i].astype(BF16), w_proj_ssm[i].astype(BF16),
                     w_out[i].astype(BF16), tm=min(tm, 512))

        h2, s1, s2, stats = peer_keys(x, norm_ffn[i].reshape(1, D_MODEL), peer_wq[i].T.astype(BF16),
                                      peer_k1[i].astype(BF16), peer_k2[i].astype(BF16), tt=tt)
        x = peer_mix(x, h2, peer_u[i].astype(BF16), peer_v[i].T.astype(BF16), s1, s2, stats, tt=tt, eb=1024)

        outs["kp"].append(k_rot[:tp].reshape(bp, lp, N_HEADS, 2, HEAD_DIM))
        outs["vp"].append(v_f32[:tp].reshape(bp, lp, N_HEADS, 2 * HEAD_DIM))
        outs["sp"].append(ssm_p)
        outs["cp"].append(conv_p)
        outs["ks"].append(k_rot[tp:].reshape(bs, ls, N_HEADS, 2, HEAD_DIM))
        outs["vs"].append(v_f32[tp:].reshape(bs, ls, N_HEADS, 2 * HEAD_DIM))
        outs["ss"].append(ssm_s)
        outs["cs"].append(conv_s)

    stack = lambda name: jnp.stack(outs[name])
    return (x[:tp].reshape(bp, lp, D_MODEL), x[tp:].reshape(bs, ls, D_MODEL),
            stack("kp"), stack("vp"), stack("sp"), stack("cp"),
            stack("ks"), stack("vs"), stack("ss"), stack("cs"))
```

```python
import functools
import math

import jax
import jax.numpy as jnp
from jax import lax
from jax.experimental import pallas as pl
from jax.experimental.pallas import tpu as pltpu

F32 = jnp.float32
BF16 = jnp.bfloat16

D_MODEL = 1024
DEPTH = 4
CHUNK = 64
N_HEADS = 8
HEAD_DIM = 64
ATT_DIM = N_HEADS * 2 * HEAD_DIM
ROPE_THETA = 10000.0
D_INNER = 2048
SSM_HEAD_DIM = 64
SSM_HEADS = D_INNER // SSM_HEAD_DIM
SSM_GROUPS = 8
HEADS_PER_GROUP = SSM_HEADS // SSM_GROUPS
D_STATE = 128
CONV_W = 4
CONV_DIM = D_INNER + 2 * SSM_GROUPS * D_STATE
GATE_DIM = 2 * D_MODEL
PK_HEADS = 8
N_KEYS = 128
N_EXPERTS = N_KEYS * N_KEYS
PK_TOPK = 16
PK_HALF = 128
EPS = 1e-6

LANES = 128
SUBLANES = 8
VMEM_LIMIT = 56 * 1024 * 1024

COL_Z = 0
COL_GATE = 2048
COL_XBC = 4096
COL_Q = 8192
COL_K = 9216
COL_V = 10240
COL_DT = 11264
PROJ_COLS = 11520
PROJ_TN = 1152
ATT_TQ = 512
ATT_TK = 512
ATT_TKS = 512
PEER_SUB = 512
PEER_EB = 4096

HIGHEST = lax.Precision.HIGHEST
NT_DIMS = (((1,), (1,)), ((), ()))


def _params(sem):
    return pltpu.CompilerParams(dimension_semantics=sem, vmem_limit_bytes=VMEM_LIMIT)


def _dot(a, b):
    return jnp.dot(a, b, preferred_element_type=F32)


def _dot_nt(a, b):
    return lax.dot_general(a, b, NT_DIMS, preferred_element_type=F32)


def _dot_exact(a, b):
    return jnp.dot(a, b, preferred_element_type=F32, precision=HIGHEST)


def _transpose_exact(x):
    c = x.shape[1]
    eye = (lax.broadcasted_iota(jnp.int32, (c, c), 0) == lax.broadcasted_iota(jnp.int32, (c, c), 1)).astype(F32)
    return lax.dot_general(eye, x, NT_DIMS, preferred_element_type=F32, precision=HIGHEST)


def _sigmoid(x):
    return 1.0 / (1.0 + jnp.exp(-x))


def _silu(x):
    return x * _sigmoid(x)


def _softplus(x):
    return jnp.maximum(x, 0.0) + jnp.log(1.0 + jnp.exp(-jnp.abs(x)))


def _in_proj_kernel(x_ref, g_ref, w_ref, o_ref, h_scr):
    @pl.when(pl.program_id(1) == 0)
    def _():
        x = x_ref[...]
        ms = jnp.mean(x * x, axis=-1, keepdims=True)
        h_scr[...] = (x * lax.rsqrt(ms + EPS) * g_ref[...]).astype(BF16)

    o_ref[...] = _dot(h_scr[...], w_ref[...])


def in_proj(x, gain, w, *, tm):
    t = x.shape[0]
    n = w.shape[1]
    return pl.pallas_call(
        _in_proj_kernel,
        out_shape=jax.ShapeDtypeStruct((t, n), F32),
        grid=(t // tm, n // PROJ_TN),
        in_specs=[pl.BlockSpec((tm, D_MODEL), lambda i, j: (i, 0)),
                  pl.BlockSpec((1, D_MODEL), lambda i, j: (0, 0)),
                  pl.BlockSpec((D_MODEL, PROJ_TN), lambda i, j: (0, j))],
        out_specs=pl.BlockSpec((tm, PROJ_TN), lambda i, j: (i, j)),
        scratch_shapes=[pltpu.VMEM((tm, D_MODEL), BF16)],
        compiler_params=_params(("parallel", "arbitrary")),
        name="in_proj",
    )(x, gain, w)


def _qk_prep_kernel(q_ref, k_ref, v_ref, cos_ref, sin_ref, qg_ref, kg_ref,
                    qb_ref, ko_ref, kb_ref, vo_ref, vb_ref):
    cos = cos_ref[...]
    sin = sin_ref[...]
    row = lax.broadcasted_iota(jnp.int32, (LANES, LANES), 0) // HEAD_DIM
    col = lax.broadcasted_iota(jnp.int32, (LANES, LANES), 1) // HEAD_DIM
    group_mean = jnp.where(row == col, 1.0 / HEAD_DIM, 0.0).astype(F32)
    lane = lax.broadcasted_iota(jnp.int32, (1, LANES), 1)
    low_half = (lane % HEAD_DIM) < (HEAD_DIM // 2)

    def norm_rope(x, gain):
        ms = _dot_exact(x * x, group_mean)
        y = x * lax.rsqrt(ms + EPS) * gain
        partner = jnp.where(low_half, pltpu.roll(y, LANES - HEAD_DIM // 2, 1), pltpu.roll(y, HEAD_DIM // 2, 1))
        return y * cos + partner * sin

    for h in range(N_HEADS):
        sl = slice(h * LANES, (h + 1) * LANES)
        q = norm_rope(q_ref[:, sl], qg_ref[...]) * (HEAD_DIM ** -0.5)
        qb_ref[:, sl] = q.astype(BF16)
        k = norm_rope(k_ref[:, sl], kg_ref[...])
        ko_ref[:, sl] = k
        kb_ref[:, sl] = k.astype(BF16)
    v = v_ref[...]
    vo_ref[...] = v
    vb_ref[...] = v.astype(BF16)


def qk_prep(proj, cos, sin, q_gain, k_gain, *, tm):
    t = proj.shape[0]
    blk = lambda c: pl.BlockSpec((tm, ATT_DIM), lambda i, c=c: (i, c // ATT_DIM))
    tab = pl.BlockSpec((tm, LANES), lambda i: (i, 0))
    vec = pl.BlockSpec((1, LANES), lambda i: (0, 0))
    out = pl.BlockSpec((tm, ATT_DIM), lambda i: (i, 0))
    return pl.pallas_call(
        _qk_prep_kernel,
        out_shape=[jax.ShapeDtypeStruct((t, ATT_DIM), BF16),
                   jax.ShapeDtypeStruct((t, ATT_DIM), F32),
                   jax.ShapeDtypeStruct((t, ATT_DIM), BF16),
                   jax.ShapeDtypeStruct((t, ATT_DIM), F32),
                   jax.ShapeDtypeStruct((t, ATT_DIM), BF16)],
        grid=(t // tm,),
        in_specs=[blk(COL_Q), blk(COL_K), blk(COL_V), tab, tab, vec, vec],
        out_specs=[out, out, out, out, out],
        compiler_params=_params(("parallel",)),
        name="qk_prep",
    )(proj, proj, proj, cos, sin, q_gain, k_gain)


def _lambda_full(lam_ref, lam_init):
    a = jnp.sum(lam_ref[0:1, :] * lam_ref[1:2, :], axis=-1, keepdims=True)
    b = jnp.sum(lam_ref[2:3, :] * lam_ref[3:4, :], axis=-1, keepdims=True)
    return jnp.exp(a) - jnp.exp(b) + lam_init


def _split_components(q):
    lane = lax.broadcasted_iota(jnp.int32, (1, LANES), 1)
    zero = jnp.zeros_like(q)
    return jnp.concatenate([jnp.where(lane < HEAD_DIM, q, zero), jnp.where(lane >= HEAD_DIM, q, zero)], axis=0)


def _flash_update(s, v, m_ref, l_ref, acc_ref):
    m_old = m_ref[...]
    m_new = jnp.maximum(m_old, jnp.max(s, axis=-1, keepdims=True))
    alpha = jnp.exp(m_old - m_new)
    p = jnp.exp(s - m_new)
    l_ref[...] = alpha * l_ref[...] + jnp.sum(p, axis=-1, keepdims=True)
    acc_ref[...] = alpha * acc_ref[...] + _dot(p.astype(BF16), v)
    m_ref[...] = m_new


def _diff_combine(acc, l, rows, lam, gain, lam_init):
    o1 = acc[:rows] / l[:rows]
    o2 = acc[rows:] / l[rows:]
    o = o1 - lam * o2
    ms = jnp.mean(o * o, axis=-1, keepdims=True)
    return o * lax.rsqrt(ms + EPS) * gain * (1.0 - lam_init)


def _attn_prompt_kernel(lam_ref, g_ref, q_ref, k_ref, v_ref, o_ref,
                        vext_scr, s_scr, p_scr, m_scr, alpha_scr, acc_scr, *, tq, tk, lam_init):
    i = pl.program_id(2)
    rows = 2 * tq

    @pl.when(i == 0)
    def _():
        vext_scr[:, :LANES] = v_ref[...]
        vext_scr[:, LANES:] = jnp.ones((vext_scr.shape[0], LANES), BF16)

    qq = _split_components(q_ref[...])
    diag = (i * tq) // tk
    n_blocks = diag + 1

    def key_block(t):
        return jnp.where(t == 0, diag, t - 1)

    def scores(blk):
        return _dot_nt(qq, k_ref[pl.ds(pl.multiple_of(blk * tk, tk), tk), :])

    def softmax(slot, bias=None):
        s = s_scr[slot]
        if bias is not None:
            s = s + bias
        m_old = m_scr[...]
        m_new = jnp.maximum(m_old, jnp.max(s, axis=-1, keepdims=True))
        alpha_scr[slot] = jnp.exp(m_old - m_new)
        for c in range(tk // LANES):
            sl = slice(c * LANES, (c + 1) * LANES)
            p_scr[slot, :, sl] = jnp.exp(s[:, sl] - m_new).astype(BF16)
        m_scr[...] = m_new

    def accumulate(slot, blk):
        pv = _dot(p_scr[slot], vext_scr[pl.ds(pl.multiple_of(blk * tk, tk), tk), :])
        alpha = alpha_scr[slot]
        acc_scr[:, :LANES] = alpha * acc_scr[:, :LANES] + pv[:, :LANES]
        acc_scr[:, LANES:] = alpha * acc_scr[:, LANES:] + pv[:, LANES:]

    m_scr[...] = jnp.full(m_scr.shape, -jnp.inf, F32)
    acc_scr[...] = jnp.zeros(acc_scr.shape, F32)
    p_scr[1] = jnp.zeros(p_scr.shape[1:], BF16)
    alpha_scr[1] = jnp.ones(alpha_scr.shape[1:], F32)

    q_pos = i * tq + lax.broadcasted_iota(jnp.int32, (rows, tk), 0) % tq
    k_pos = diag * tk + lax.broadcasted_iota(jnp.int32, (rows, tk), 1)
    s_scr[0] = jnp.where(k_pos // CHUNK <= q_pos // CHUNK, scores(diag), -jnp.inf)

    last = n_blocks - 1

    def body(pair, carry):
        t = 2 * pair
        s_scr[1] = scores(key_block(jnp.minimum(t + 1, last)))
        softmax(0)
        accumulate(1, key_block(jnp.maximum(t - 1, 0)))
        s_scr[0] = scores(key_block(jnp.minimum(t + 2, last)))
        softmax(1, jnp.where(t + 1 <= last, 0.0, -jnp.inf))
        accumulate(0, key_block(t))
        return carry

    n_pairs = (n_blocks + 1) // 2
    lax.fori_loop(0, n_pairs, body, 0)
    accumulate(1, key_block(jnp.minimum(2 * n_pairs - 1, last)))

    lam = _lambda_full(lam_ref, lam_init)
    acc = acc_scr[...]
    o_ref[...] = _diff_combine(acc[:, :LANES], acc[:, LANES:], tq, lam, g_ref[...], lam_init).astype(o_ref.dtype)


def attn_prompt(qb, kb, vb, lam_rows, sub_gain, lam_init, *, batch, seq, tq, tk):
    nq = seq // tq
    assert tk % tq == 0 and tq % CHUNK == 0, "one key block must hold all chunks of a query block"
    kern = functools.partial(_attn_prompt_kernel, tq=tq, tk=tk, lam_init=lam_init)
    return pl.pallas_call(
        kern,
        out_shape=jax.ShapeDtypeStruct((batch * seq, ATT_DIM), BF16),
        grid=(batch, N_HEADS, nq),
        in_specs=[pl.BlockSpec((4, HEAD_DIM), lambda b, h, i: (0, 0)),
                  pl.BlockSpec((1, LANES), lambda b, h, i: (0, 0)),
                  pl.BlockSpec((tq, LANES), lambda b, h, i: (b * nq + i, h)),
                  pl.BlockSpec((seq, LANES), lambda b, h, i: (b, h)),
                  pl.BlockSpec((seq, LANES), lambda b, h, i: (b, h))],
        out_specs=pl.BlockSpec((tq, LANES), lambda b, h, i: (b * nq + i, h)),
        scratch_shapes=[pltpu.VMEM((seq, 2 * LANES), BF16),
                        pltpu.VMEM((2, 2 * tq, tk), F32), pltpu.VMEM((2, 2 * tq, tk), BF16),
                        pltpu.VMEM((2 * tq, LANES), F32), pltpu.VMEM((2, 2 * tq, LANES), F32),
                        pltpu.VMEM((2 * tq, 2 * LANES), F32)],
        compiler_params=_params(("parallel", "parallel", "arbitrary")),
        name="attn_prompt",
    )(lam_rows, sub_gain, qb, kb, vb)


def _attn_sample_kernel(lam_ref, g_ref, q_ref, kn_ref, vn_ref, kc_ref, vc_ref, o_ref,
                        m_scr, l_scr, acc_scr, *, rows, lam_init):
    j = pl.program_id(1)

    @pl.when(j == 0)
    def _():
        m_scr[...] = jnp.full(m_scr.shape, -jnp.inf, F32)
        l_scr[...] = jnp.zeros(l_scr.shape, F32)
        acc_scr[...] = jnp.zeros(acc_scr.shape, F32)

    def update(k_all, v_all):
        for h in range(N_HEADS):
            sl = slice(h * LANES, (h + 1) * LANES)
            qq = _split_components(q_ref[:, sl])
            s = _dot_nt(qq, k_all[:, sl].astype(BF16))
            _flash_update(s, v_all[:, sl].astype(BF16), m_scr.at[h], l_scr.at[h], acc_scr.at[h])

    update(kc_ref[...], vc_ref[...])

    @pl.when(j == pl.num_programs(1) - 1)
    def _():
        update(kn_ref[...], vn_ref[...])
        lam = _lambda_full(lam_ref, lam_init)
        for h in range(N_HEADS):
            sl = slice(h * LANES, (h + 1) * LANES)
            o_ref[:, sl] = _diff_combine(acc_scr[h], l_scr[h], rows, lam, g_ref[...], lam_init).astype(o_ref.dtype)


def attn_sample(qb, kb, vb, cache_k, cache_v, lam_rows, sub_gain, lam_init, *, layer, batch, rows, row0, tk):
    past = cache_k.shape[2]
    new_blk = pl.BlockSpec((rows, ATT_DIM), lambda b, j: (row0 // rows + b, 0))
    cache_blk = pl.BlockSpec((None, None, tk, ATT_DIM), lambda b, j: (layer, b, j, 0))
    kern = functools.partial(_attn_sample_kernel, rows=rows, lam_init=lam_init)
    return pl.pallas_call(
        kern,
        out_shape=jax.ShapeDtypeStruct((batch * rows, ATT_DIM), BF16),
        grid=(batch, past // tk),
        in_specs=[pl.BlockSpec((4, HEAD_DIM), lambda b, j: (0, 0)),
                  pl.BlockSpec((1, LANES), lambda b, j: (0, 0)),
                  new_blk, new_blk, new_blk, cache_blk, cache_blk],
        out_specs=pl.BlockSpec((rows, ATT_DIM), lambda b, j: (b, 0)),
        scratch_shapes=[pltpu.VMEM((N_HEADS, 2 * rows, 1), F32), pltpu.VMEM((N_HEADS, 2 * rows, 1), F32),
                        pltpu.VMEM((N_HEADS, 2 * rows, LANES), F32)],
        compiler_params=_params(("parallel", "arbitrary")),
        name="attn_sample",
    )(lam_rows, sub_gain, qb, kb, vb, cache_k, cache_v)


def _ssd_kernel(xbc_ref, z_ref, dt_ref, conv0_ref, ssm0_ref, cw_ref, cb_ref, dtb_ref, alog_ref, dskip_ref, ng_ref,
                y_ref, ssm_ref, conv_ref, xpad_scr, *, lc):
    c = pl.program_id(1)
    pad0 = SUBLANES - (CONV_W - 1)

    @pl.when(c == 0)
    def _():
        xpad_scr[pad0:SUBLANES, :] = conv0_ref[...]
        ssm_ref[...] = ssm0_ref[...]

    xpad_scr[SUBLANES:, :] = xbc_ref[...]
    conv = cb_ref[...] + sum(xpad_scr[pad0 + j:pad0 + j + lc, :] * cw_ref[j:j + 1, :] for j in range(CONV_W))
    conv = _silu(conv)
    tail = xpad_scr[lc + pad0:lc + SUBLANES, :]
    xpad_scr[pad0:SUBLANES, :] = tail
    conv_ref[...] = tail

    dt = _softplus(dt_ref[...] + dtb_ref[...])
    da = dt * (-jnp.exp(alog_ref[...]))
    row = lax.broadcasted_iota(jnp.int32, (lc, lc), 0)
    col = lax.broadcasted_iota(jnp.int32, (lc, lc), 1)
    causal = row >= col
    a_cum = _dot_exact(causal.astype(F32), da)
    dt_t = _transpose_exact(dt)
    a_cum_t = _transpose_exact(a_cum)
    eye = (lax.broadcasted_iota(jnp.int32, (LANES, LANES), 0)
           == lax.broadcasted_iota(jnp.int32, (LANES, LANES), 1)).astype(BF16)

    for g in range(SSM_GROUPS):
        b_g = conv[:, D_INNER + g * D_STATE:D_INNER + (g + 1) * D_STATE].astype(BF16)
        c_off = D_INNER + SSM_GROUPS * D_STATE
        c_g = conv[:, c_off + g * D_STATE:c_off + (g + 1) * D_STATE].astype(BF16)
        cb = _dot_nt(c_g, b_g)
        ys = []
        for pair in range(HEADS_PER_GROUP // 2):
            lane0 = (g * HEADS_PER_GROUP + 2 * pair) * SSM_HEAD_DIM
            x_pair = conv[:, lane0:lane0 + LANES].astype(BF16)
            x_pair_t = _dot_nt(eye, x_pair)
            for sub in range(2):
                h = g * HEADS_PER_GROUP + 2 * pair + sub
                x_h = x_pair[:, sub * SSM_HEAD_DIM:(sub + 1) * SSM_HEAD_DIM]
                x_h_t = x_pair_t[sub * SSM_HEAD_DIM:(sub + 1) * SSM_HEAD_DIM, :]
                a_col = a_cum[:, h:h + 1]
                a_row = a_cum_t[h:h + 1, :]
                dt_row = dt_t[h:h + 1, :]
                a_last = a_cum_t[h:h + 1, lc - 1:lc]
                decay = jnp.exp(jnp.where(causal, a_col - a_row, -jnp.inf))
                w_in = (cb * decay * dt_row).astype(BF16)
                state = ssm_ref[h]
                y_h = _dot(w_in, x_h) + jnp.exp(a_col) * _dot_nt(c_g, state.astype(BF16))
                ys.append(y_h)
                w_state = (x_h_t * (dt_row * jnp.exp(a_last - a_row))).astype(BF16)
                ssm_ref[h] = jnp.exp(a_last) * state + _dot(w_state, b_g)
        lanes = slice(g * HEADS_PER_GROUP * SSM_HEAD_DIM, (g + 1) * HEADS_PER_GROUP * SSM_HEAD_DIM)
        y_g = jnp.concatenate(ys, axis=-1) + dskip_ref[:, lanes] * conv[:, lanes]
        y_g = y_g * _silu(z_ref[:, lanes])
        ms = jnp.mean(y_g * y_g, axis=-1, keepdims=True)
        y_ref[:, lanes] = (y_g * lax.rsqrt(ms + EPS) * ng_ref[:, lanes]).astype(y_ref.dtype)


def ssd(proj, conv0, ssm0, conv_w, conv_b, dt_bias, a_log, d_skip, norm_gain, *, batch, seq, row0, lc):
    nc = seq // lc
    tok = lambda width, col: pl.BlockSpec((lc, width), lambda b, c: (row0 // lc + b * nc + c, col // width))
    const = lambda r, w: pl.BlockSpec((r, w), lambda b, c: (0, 0))
    kern = functools.partial(_ssd_kernel, lc=lc)
    return pl.pallas_call(
        kern,
        out_shape=[jax.ShapeDtypeStruct((batch * seq, D_INNER), BF16),
                   jax.ShapeDtypeStruct((batch, SSM_HEADS, SSM_HEAD_DIM, D_STATE), F32),
                   jax.ShapeDtypeStruct((batch, CONV_W - 1, CONV_DIM), F32)],
        grid=(batch, nc),
        in_specs=[tok(CONV_DIM, COL_XBC), tok(D_INNER, COL_Z), tok(LANES, COL_DT),
                  pl.BlockSpec((None, CONV_W - 1, CONV_DIM), lambda b, c: (b, 0, 0)),
                  pl.BlockSpec((None, SSM_HEADS, SSM_HEAD_DIM, D_STATE), lambda b, c: (b, 0, 0, 0)),
                  const(CONV_W, CONV_DIM), const(1, CONV_DIM), const(1, LANES), const(1, LANES),
                  const(1, D_INNER), const(1, D_INNER)],
        out_specs=[pl.BlockSpec((lc, D_INNER), lambda b, c: (b * nc + c, 0)),
                   pl.BlockSpec((None, SSM_HEADS, SSM_HEAD_DIM, D_STATE), lambda b, c: (b, 0, 0, 0)),
                   pl.BlockSpec((None, CONV_W - 1, CONV_DIM), lambda b, c: (b, 0, 0))],
        scratch_shapes=[pltpu.VMEM((SUBLANES + lc, CONV_DIM), F32)],
        compiler_params=_params(("parallel", "arbitrary")),
        name="ssd",
    )(proj, proj, proj, conv0, ssm0, conv_w, conv_b, dt_bias, a_log, d_skip, norm_gain)


def _out_proj_kernel(x_ref, att_ref, y_ref, ga_ref, gm_ref, wa_ref, wm_ref, wo_ref, o_ref):
    merged = (_sigmoid(ga_ref[...]) * _dot(att_ref[...], wa_ref[...])
              + _sigmoid(gm_ref[...]) * _dot(y_ref[...], wm_ref[...]))
    o_ref[...] = x_ref[...] + _dot(merged.astype(BF16), wo_ref[...])


def out_proj(x, att, y, proj, wa, wm, wo, *, tm):
    t = x.shape[0]
    gate = lambda c: pl.BlockSpec((tm, D_MODEL), lambda i, c=c: (i, c // D_MODEL))
    const = lambda r: pl.BlockSpec((r, D_MODEL), lambda i: (0, 0))
    return pl.pallas_call(
        _out_proj_kernel,
        out_shape=jax.ShapeDtypeStruct((t, D_MODEL), F32),
        grid=(t // tm,),
        in_specs=[pl.BlockSpec((tm, D_MODEL), lambda i: (i, 0)),
                  pl.BlockSpec((tm, ATT_DIM), lambda i: (i, 0)),
                  pl.BlockSpec((tm, D_INNER), lambda i: (i, 0)),
                  gate(COL_GATE), gate(COL_GATE + D_MODEL),
                  const(ATT_DIM), const(D_INNER), const(D_MODEL)],
        out_specs=pl.BlockSpec((tm, D_MODEL), lambda i: (i, 0)),
        compiler_params=_params(("parallel",)),
        name="out_proj",
    )(x, att, y, proj, proj, wa, wm, wo)


def _top_ranked(s, k):
    vals = []
    rank = jnp.full(s.shape, float(k), F32)
    for i in range(k):
        m = jnp.max(s, axis=0, keepdims=True)
        hit = s == m
        vals.append(m)
        rank = jnp.where(hit, float(i), rank)
        s = jnp.where(hit, -jnp.inf, s)
    return vals, rank


def _peer_keys_kernel(x_ref, g_ref, wqt_ref, k1_ref, k2_ref, h_ref, e1_ref, cnt_ref, e2_ref, r2_ref):
    x = x_ref[...]
    ms = jnp.mean(x * x, axis=-1, keepdims=True)
    h = (x * lax.rsqrt(ms + EPS) * g_ref[...]).astype(BF16)
    h_ref[...] = h
    q_t = _dot_nt(wqt_ref[...], h)
    for hd in range(PK_HEADS):
        r0 = hd * 2 * PK_HALF
        s1 = _dot(k1_ref[hd], q_t[r0:r0 + PK_HALF, :].astype(BF16))
        s2 = _dot(k2_ref[hd], q_t[r0 + PK_HALF:r0 + 2 * PK_HALF, :].astype(BF16))
        v1, r1 = _top_ranked(s1, PK_TOPK)
        v2, r2 = _top_ranked(s2, PK_TOPK)
        v2 = jnp.concatenate(v2, axis=0)
        cand = jnp.concatenate([a + v2 for a in v1], axis=0)
        tau = _top_ranked(cand, PK_TOPK)[0][-1]
        top = v1[0] + v2[0:1, :]
        chosen = cand >= tau
        z = jnp.sum(jnp.where(chosen, jnp.exp(cand - top), 0.0), axis=0, keepdims=True)
        cnt = jnp.zeros(s1.shape, F32)
        for a in range(PK_TOPK):
            n_a = jnp.sum(chosen[a * PK_TOPK:(a + 1) * PK_TOPK].astype(F32), axis=0, keepdims=True)
            cnt = jnp.where(r1 == float(a), n_a, cnt)
        e1_ref[hd] = jnp.exp(s1 - v1[0]) / z
        cnt_ref[hd] = cnt
        e2_ref[hd] = jnp.exp(s2 - v2[0:1, :])
        r2_ref[hd] = r2


def peer_keys(x, gain, wq_t, k1, k2, *, tt):
    t = x.shape[0]
    per_head = pl.BlockSpec((PK_HEADS, N_KEYS, tt), lambda i: (0, 0, i))
    table = jax.ShapeDtypeStruct((PK_HEADS, N_KEYS, t), F32)
    return pl.pallas_call(
        _peer_keys_kernel,
        out_shape=[jax.ShapeDtypeStruct((t, D_MODEL), BF16), table, table, table, table],
        grid=(t // tt,),
        in_specs=[pl.BlockSpec((tt, D_MODEL), lambda i: (i, 0)),
                  pl.BlockSpec((1, D_MODEL), lambda i: (0, 0)),
                  pl.BlockSpec((PK_HEADS * 2 * PK_HALF, D_MODEL), lambda i: (0, 0)),
                  pl.BlockSpec((PK_HEADS, N_KEYS, PK_HALF), lambda i: (0, 0, 0)),
                  pl.BlockSpec((PK_HEADS, N_KEYS, PK_HALF), lambda i: (0, 0, 0))],
        out_specs=[pl.BlockSpec((tt, D_MODEL), lambda i: (i, 0)), per_head, per_head, per_head, per_head],
        compiler_params=_params(("parallel",)),
        name="peer_keys",
    )(x, gain, wq_t, k1, k2)


def _gelu(x):
    return 0.5 * x * (1.0 + lax.erf(x * (2.0 ** -0.5)))


def _peer_mix_kernel(x_ref, h_ref, u_ref, vt_ref, e1_ref, cnt_ref, e2_ref, r2_ref, o_ref,
                     acc_scr, raw_scr, coef_scr, *, eb):
    e = pl.program_id(1)
    tt = acc_scr.shape[1]
    n_sub = eb // PEER_SUB
    rows_per_sub = PEER_SUB // N_KEYS
    pack = 2 * SUBLANES

    @pl.when(e == 0)
    def _():
        acc_scr[...] = jnp.zeros(acc_scr.shape, F32)

    def pre_activations(j):
        start = pl.multiple_of(j * PEER_SUB, PEER_SUB)
        return _dot_nt(u_ref[pl.ds(start, PEER_SUB), :], h_ref[...])

    def mix(slot, j):
        for r in range(rows_per_sub):
            i1 = (e * n_sub + j) * rows_per_sub + r
            cnt_rows = [cnt_ref[hd, pl.ds(i1, 1), :] for hd in range(PK_HEADS)]
            e1_rows = [e1_ref[hd, pl.ds(i1, 1), :] for hd in range(PK_HEADS)]
            for lt in range(tt // LANES):
                lanes = slice(lt * LANES, (lt + 1) * LANES)
                cnt_b = [jnp.broadcast_to(row[:, lanes], (SUBLANES, LANES)) for row in cnt_rows]
                e1_b = [jnp.broadcast_to(row[:, lanes], (SUBLANES, LANES)) for row in e1_rows]
                for c in range(N_KEYS // pack):
                    tiles = []
                    for half in range(2):
                        k0 = c * pack + half * SUBLANES
                        keys = slice(k0, k0 + SUBLANES)
                        gate = jnp.zeros((SUBLANES, LANES), F32)
                        for hd in range(PK_HEADS):
                            keep = r2_ref[hd, keys, lanes] < cnt_b[hd]
                            gate = gate + jnp.where(keep, e2_ref[hd, keys, lanes] * e1_b[hd], 0.0)
                        tiles.append(gate * _gelu(raw_scr[slot, r * N_KEYS + k0:r * N_KEYS + k0 + SUBLANES, lanes]))
                    rows = slice(r * N_KEYS + c * pack, r * N_KEYS + (c + 1) * pack)
                    coef_scr[slot, rows, lanes] = jnp.concatenate(tiles, axis=0).astype(BF16)

    def accumulate(slot, j):
        acc_scr[...] += _dot(vt_ref[j], coef_scr[slot])

    raw_scr[0] = pre_activations(0)
    coef_scr[1] = jnp.zeros(coef_scr.shape[1:], BF16)

    def body(pair, carry):
        j = 2 * pair
        raw_scr[1] = pre_activations(j + 1)
        mix(0, j)
        accumulate(1, jnp.maximum(j - 1, 0))
        raw_scr[0] = pre_activations(jnp.minimum(j + 2, n_sub - 1))
        mix(1, j + 1)
        accumulate(0, j)
        return carry

    lax.fori_loop(0, n_sub // 2, body, 0)
    accumulate(1, n_sub - 1)

    @pl.when(e == pl.num_programs(1) - 1)
    def _():
        o_ref[...] = x_ref[...] + acc_scr[...].T


def peer_mix(x, h, u, v_sub, e1, cnt, e2, r2, *, tt, eb):
    t = x.shape[0]
    assert (eb // PEER_SUB) % 2 == 0
    per_head = pl.BlockSpec((PK_HEADS, N_KEYS, tt), lambda i, e: (0, 0, i))
    kern = functools.partial(_peer_mix_kernel, eb=eb)
    return pl.pallas_call(
        kern,
        out_shape=jax.ShapeDtypeStruct((t, D_MODEL), F32),
        grid=(t // tt, N_EXPERTS // eb),
        in_specs=[pl.BlockSpec((tt, D_MODEL), lambda i, e: (i, 0)),
                  pl.BlockSpec((tt, D_MODEL), lambda i, e: (i, 0)),
                  pl.BlockSpec((eb, D_MODEL), lambda i, e: (e, 0)),
                  pl.BlockSpec((eb // PEER_SUB, D_MODEL, PEER_SUB), lambda i, e: (e, 0, 0)),
                  per_head, per_head, per_head, per_head],
        out_specs=pl.BlockSpec((tt, D_MODEL), lambda i, e: (i, 0)),
        scratch_shapes=[pltpu.VMEM((D_MODEL, tt), F32), pltpu.VMEM((2, PEER_SUB, tt), F32),
                        pltpu.VMEM((2, PEER_SUB, tt), BF16)],
        compiler_params=_params(("parallel", "arbitrary")),
        name="peer_mix",
    )(x, h, u, v_sub, e1, cnt, e2, r2)


def _rope_tables(pos):
    half = HEAD_DIM // 2
    inv_freq = ROPE_THETA ** (-jnp.arange(half, dtype=F32) / half)
    ang = pos[:, None] * inv_freq[None, :]
    cos, sin = jnp.cos(ang), jnp.sin(ang)
    reps = LANES // HEAD_DIM
    return jnp.tile(jnp.concatenate([cos, cos], -1), (1, reps)), jnp.tile(jnp.concatenate([-sin, sin], -1), (1, reps))


def _lambda_init(layer):
    return 0.8 - 0.6 * math.exp(-0.3 * layer)


def _pad_lanes(v, width=LANES):
    return jnp.pad(v, (0, width - v.shape[0])).reshape(1, width)


def kernel(x_prompt, x_sample, cache_k, cache_v, state_ssm, state_conv, norm_mix, w_in, q_norm, k_norm, lam_q1, lam_k1, lam_q2, lam_k2, attn_subln, conv_w, conv_b, dt_bias, a_log, d_skip, ssm_norm, w_proj_attn, w_proj_ssm, w_out, norm_ffn, peer_wq, peer_k1, peer_k2, peer_u, peer_v):
    bp, lp, _ = x_prompt.shape
    bs, ls, _ = x_sample.shape
    depth = w_in.shape[0]
    n_past = cache_k.shape[2]
    tp, ts = bp * lp, bs * ls
    x = jnp.concatenate([x_prompt.reshape(tp, D_MODEL), x_sample.reshape(ts, D_MODEL)], axis=0)
    t_all = tp + ts
    tm = math.gcd(t_all, 1024)
    tt = math.gcd(t_all, 256)

    pos = jnp.concatenate([jnp.tile(jnp.arange(lp, dtype=F32), bp),
                           jnp.tile(n_past + jnp.arange(ls, dtype=F32), bs)])
    cos, sin = _rope_tables(pos)
    cache_k = cache_k.reshape(depth, bs, n_past, ATT_DIM)
    cache_v = cache_v.reshape(depth, bs, n_past, ATT_DIM)
    conv0_p = jnp.zeros((bp, CONV_W - 1, CONV_DIM), F32)
    ssm0_p = jnp.zeros((bp, SSM_HEADS, SSM_HEAD_DIM, D_STATE), F32)

    outs = {name: [] for name in ("kp", "vp", "sp", "cp", "ks", "vs", "ss", "cs")}
    for i in range(depth):
        li = _lambda_init(i)
        q_w, k_w, v_w, z_w, xbc_w, dt_w, gate_w = jnp.split(
            w_in[i], [int(o) for o in (1024, 2048, 3072, 5120, 9216, 9248)], axis=-1)
        w_all = jnp.concatenate([z_w, gate_w, xbc_w, q_w, k_w, v_w, dt_w,
                                 jnp.zeros((D_MODEL, PROJ_COLS - COL_DT - SSM_HEADS), F32)], axis=-1).astype(BF16)
        proj = in_proj(x, norm_mix[i].reshape(1, D_MODEL), w_all, tm=tm)

        q_gain = jnp.tile(q_norm[i], LANES // HEAD_DIM).reshape(1, LANES)
        k_gain = jnp.tile(k_norm[i], LANES // HEAD_DIM).reshape(1, LANES)
        qb, k_rot, kb, v_f32, vb = qk_prep(proj, cos, sin, q_gain, k_gain, tm=min(tm, 256))

        lam_rows = jnp.stack([lam_q1[i], lam_k1[i], lam_q2[i], lam_k2[i]])
        sub_gain = attn_subln[i].reshape(1, LANES)
        att_p = attn_prompt(qb, kb, vb, lam_rows, sub_gain, li, batch=bp, seq=lp,
                            tq=min(ATT_TQ, lp), tk=min(ATT_TK, lp))
        att_s = attn_sample(qb, kb, vb, cache_k, cache_v, lam_rows, sub_gain, li,
                            layer=i, batch=bs, rows=ls, row0=tp, tk=min(ATT_TKS, n_past))
        att = jnp.concatenate([att_p, att_s], axis=0)

        ssd_w = (conv_w[i], conv_b[i].reshape(1, CONV_DIM), _pad_lanes(dt_bias[i]), _pad_lanes(a_log[i]),
                 jnp.repeat(d_skip[i], SSM_HEAD_DIM).reshape(1, D_INNER), ssm_norm[i].reshape(1, D_INNER))
        y_p, ssm_p, conv_p = ssd(proj, conv0_p, ssm0_p, *ssd_w, batch=bp, seq=lp, row0=0, lc=min(CHUNK, lp))
        y_s, ssm_s, conv_s = ssd(proj, state_conv[i], state_ssm[i], *ssd_w, batch=bs, seq=ls, row0=tp,
                                 lc=min(CHUNK, ls))
        y = jnp.concatenate([y_p, y_s], axis=0)

        x = out_proj(x, att, y, proj, w_proj_attn[i].astype(BF16), w_proj_ssm[i].astype(BF16),
                     w_out[i].astype(BF16), tm=min(tm, 512))

        h2, *tables = peer_keys(x, norm_ffn[i].reshape(1, D_MODEL), peer_wq[i].T.astype(BF16),
                                peer_k1[i].astype(BF16), peer_k2[i].astype(BF16), tt=tt)
        v_sub = peer_v[i].astype(BF16).reshape(N_EXPERTS // PEER_SUB, PEER_SUB, D_MODEL).transpose(0, 2, 1)
        x = peer_mix(x, h2, peer_u[i].astype(BF16), v_sub, *tables, tt=tt, eb=PEER_EB)

        outs["kp"].append(k_rot[:tp].reshape(bp, lp, N_HEADS, 2, HEAD_DIM))
        outs["vp"].append(v_f32[:tp].reshape(bp, lp, N_HEADS, 2 * HEAD_DIM))
        outs["sp"].append(ssm_p)
        outs["cp"].append(conv_p)
        outs["ks"].append(k_rot[tp:].reshape(bs, ls, N_HEADS, 2, HEAD_DIM))
        outs["vs"].append(v_f32[tp:].reshape(bs, ls, N_HEADS, 2 * HEAD_DIM))
        outs["ss"].append(ssm_s)
        outs["cs"].append(conv_s)

    stack = lambda name: jnp.stack(outs[name])
    return (x[:tp].reshape(bp, lp, D_MODEL), x[tp:].reshape(bs, ls, D_MODEL),
            stack("kp"), stack("vp"), stack("sp"), stack("cp"),
            stack("ks"), stack("vs"), stack("ss"), stack("cs"))
```

```python
import functools
import math

import jax
import jax.numpy as jnp
from jax import lax
from jax.experimental import pallas as pl
from jax.experimental.pallas import tpu as pltpu

F32 = jnp.float32
BF16 = jnp.bfloat16

D_MODEL = 1024
DEPTH = 4
CHUNK = 64
N_HEADS = 8
HEAD_DIM = 64
ATT_DIM = N_HEADS * 2 * HEAD_DIM
ROPE_THETA = 10000.0
D_INNER = 2048
SSM_HEAD_DIM = 64
SSM_HEADS = D_INNER // SSM_HEAD_DIM
SSM_GROUPS = 8
HEADS_PER_GROUP = SSM_HEADS // SSM_GROUPS
D_STATE = 128
CONV_W = 4
CONV_DIM = D_INNER + 2 * SSM_GROUPS * D_STATE
GATE_DIM = 2 * D_MODEL
PK_HEADS = 8
N_KEYS = 128
N_EXPERTS = N_KEYS * N_KEYS
PK_TOPK = 16
PK_HALF = 128
EPS = 1e-6

LANES = 128
SUBLANES = 8
VMEM_LIMIT = 56 * 1024 * 1024

COL_Z = 0
COL_GATE = 2048
COL_XBC = 4096
COL_DT = 8192
PROJ_COLS = 8320
PROJ_TN = 1664
ATT_TQ = 512
ATT_TK = 512
ATT_TKS = 512
PEER_SUB = 256
PEER_EB = 4096

HIGHEST = lax.Precision.HIGHEST
NT_DIMS = (((1,), (1,)), ((), ()))


def _params(sem):
    return pltpu.CompilerParams(dimension_semantics=sem, vmem_limit_bytes=VMEM_LIMIT)


def _dot(a, b):
    return jnp.dot(a, b, preferred_element_type=F32)


def _dot_nt(a, b):
    return lax.dot_general(a, b, NT_DIMS, preferred_element_type=F32)


def _dot_exact(a, b):
    return jnp.dot(a, b, preferred_element_type=F32, precision=HIGHEST)


def _transpose_exact(x):
    c = x.shape[1]
    eye = (lax.broadcasted_iota(jnp.int32, (c, c), 0) == lax.broadcasted_iota(jnp.int32, (c, c), 1)).astype(F32)
    return lax.dot_general(eye, x, NT_DIMS, preferred_element_type=F32, precision=HIGHEST)


def _sigmoid(x):
    return 1.0 / (1.0 + jnp.exp(-x))


def _silu(x):
    return x * _sigmoid(x)


def _softplus(x):
    return jnp.maximum(x, 0.0) + jnp.log(1.0 + jnp.exp(-jnp.abs(x)))


def _in_proj_kernel(x_ref, g_ref, w_ref, o_ref, h_scr):
    @pl.when(pl.program_id(1) == 0)
    def _():
        x = x_ref[...]
        ms = jnp.mean(x * x, axis=-1, keepdims=True)
        h_scr[...] = (x * lax.rsqrt(ms + EPS) * g_ref[...]).astype(BF16)

    o_ref[...] = _dot(h_scr[...], w_ref[...])


def in_proj(x, gain, w, *, tm):
    t = x.shape[0]
    n = w.shape[1]
    return pl.pallas_call(
        _in_proj_kernel,
        out_shape=jax.ShapeDtypeStruct((t, n), F32),
        grid=(t // tm, n // PROJ_TN),
        in_specs=[pl.BlockSpec((tm, D_MODEL), lambda i, j: (i, 0)),
                  pl.BlockSpec((1, D_MODEL), lambda i, j: (0, 0)),
                  pl.BlockSpec((D_MODEL, PROJ_TN), lambda i, j: (0, j))],
        out_specs=pl.BlockSpec((tm, PROJ_TN), lambda i, j: (i, j)),
        scratch_shapes=[pltpu.VMEM((tm, D_MODEL), BF16)],
        compiler_params=_params(("parallel", "arbitrary")),
        name="in_proj",
    )(x, gain, w)


def _qkv_proj_kernel(x_ref, g_ref, w_ref, cos_ref, sin_ref, qg_ref, kg_ref,
                     qb_ref, ko_ref, kb_ref, vo_ref, vb_ref):
    x = x_ref[...]
    ms = jnp.mean(x * x, axis=-1, keepdims=True)
    h = (x * lax.rsqrt(ms + EPS) * g_ref[...]).astype(BF16)
    qkv = _dot(h, w_ref[...])
    cos = cos_ref[...]
    sin = sin_ref[...]
    row = lax.broadcasted_iota(jnp.int32, (LANES, LANES), 0) // HEAD_DIM
    col = lax.broadcasted_iota(jnp.int32, (LANES, LANES), 1) // HEAD_DIM
    group_mean = jnp.where(row == col, 1.0 / HEAD_DIM, 0.0).astype(F32)
    lane = lax.broadcasted_iota(jnp.int32, (1, LANES), 1)
    low_half = (lane % HEAD_DIM) < (HEAD_DIM // 2)

    def norm_rope(y, gain):
        ms = _dot_exact(y * y, group_mean)
        y = y * lax.rsqrt(ms + EPS) * gain
        partner = jnp.where(low_half, pltpu.roll(y, LANES - HEAD_DIM // 2, 1), pltpu.roll(y, HEAD_DIM // 2, 1))
        return y * cos + partner * sin

    for h_i in range(N_HEADS):
        sl = slice(h_i * LANES, (h_i + 1) * LANES)
        q = norm_rope(qkv[:, h_i * LANES:(h_i + 1) * LANES], qg_ref[...]) * (HEAD_DIM ** -0.5)
        qb_ref[:, sl] = q.astype(BF16)
        k = norm_rope(qkv[:, ATT_DIM + h_i * LANES:ATT_DIM + (h_i + 1) * LANES], kg_ref[...])
        ko_ref[:, sl] = k
        kb_ref[:, sl] = k.astype(BF16)
    v = qkv[:, 2 * ATT_DIM:]
    vo_ref[...] = v
    vb_ref[...] = v.astype(BF16)


def qkv_proj(x, gain, w_qkv, cos, sin, q_gain, k_gain, *, row0, rows, tm):
    t = rows
    tile0 = row0 // tm
    tab = pl.BlockSpec((tm, LANES), lambda i: (i, 0))
    vec = pl.BlockSpec((1, LANES), lambda i: (0, 0))
    out = pl.BlockSpec((tm, ATT_DIM), lambda i: (i, 0))
    return pl.pallas_call(
        _qkv_proj_kernel,
        out_shape=[jax.ShapeDtypeStruct((t, ATT_DIM), BF16),
                   jax.ShapeDtypeStruct((t, ATT_DIM), F32),
                   jax.ShapeDtypeStruct((t, ATT_DIM), BF16),
                   jax.ShapeDtypeStruct((t, ATT_DIM), F32),
                   jax.ShapeDtypeStruct((t, ATT_DIM), BF16)],
        grid=(t // tm,),
        in_specs=[pl.BlockSpec((tm, D_MODEL), lambda i: (tile0 + i, 0)),
                  pl.BlockSpec((1, D_MODEL), lambda i: (0, 0)),
                  pl.BlockSpec((D_MODEL, 3 * ATT_DIM), lambda i: (0, 0)),
                  tab, tab, vec, vec],
        out_specs=[out, out, out, out, out],
        compiler_params=_params(("parallel",)),
        name="qkv_proj",
    )(x, gain, w_qkv, cos, sin, q_gain, k_gain)


def _lambda_full(lam_ref, lam_init):
    a = jnp.sum(lam_ref[0:1, :] * lam_ref[1:2, :], axis=-1, keepdims=True)
    b = jnp.sum(lam_ref[2:3, :] * lam_ref[3:4, :], axis=-1, keepdims=True)
    return jnp.exp(a) - jnp.exp(b) + lam_init


def _split_components(q):
    lane = lax.broadcasted_iota(jnp.int32, (1, LANES), 1)
    zero = jnp.zeros_like(q)
    return jnp.concatenate([jnp.where(lane < HEAD_DIM, q, zero), jnp.where(lane >= HEAD_DIM, q, zero)], axis=0)


def _flash_update(s, v, m_ref, l_ref, acc_ref):
    m_old = m_ref[...]
    m_new = jnp.maximum(m_old, jnp.max(s, axis=-1, keepdims=True))
    alpha = jnp.exp(m_old - m_new)
    p = jnp.exp(s - m_new)
    l_ref[...] = alpha * l_ref[...] + jnp.sum(p, axis=-1, keepdims=True)
    acc_ref[...] = alpha * acc_ref[...] + _dot(p.astype(BF16), v)
    m_ref[...] = m_new


def _diff_combine(acc, l, rows, lam, gain, lam_init):
    o1 = acc[:rows] / l[:rows]
    o2 = acc[rows:] / l[rows:]
    o = o1 - lam * o2
    ms = jnp.mean(o * o, axis=-1, keepdims=True)
    return o * lax.rsqrt(ms + EPS) * gain * (1.0 - lam_init)


def _attn_prompt_kernel(lam_ref, g_ref, q_ref, k_ref, v_ref, o_ref,
                        vext_scr, s_scr, p_scr, m_scr, alpha_scr, acc_scr, *, tq, tk, lam_init):
    i = pl.program_id(2)
    rows = 2 * tq

    @pl.when(i == 0)
    def _():
        vext_scr[:, :LANES] = v_ref[...]
        vext_scr[:, LANES:] = jnp.ones((vext_scr.shape[0], LANES), BF16)

    qq = _split_components(q_ref[...])
    diag = (i * tq) // tk
    n_blocks = diag + 1

    def key_block(t):
        return jnp.where(t == 0, diag, t - 1)

    def scores(blk):
        return _dot_nt(qq, k_ref[pl.ds(pl.multiple_of(blk * tk, tk), tk), :])

    def softmax(slot, bias=None):
        s = s_scr[slot]
        if bias is not None:
            s = s + bias
        m_old = m_scr[...]
        m_new = jnp.maximum(m_old, jnp.max(s, axis=-1, keepdims=True))
        alpha_scr[slot] = jnp.exp(m_old - m_new)
        for c in range(tk // LANES):
            sl = slice(c * LANES, (c + 1) * LANES)
            p_scr[slot, :, sl] = jnp.exp(s[:, sl] - m_new).astype(BF16)
        m_scr[...] = m_new

    def accumulate(slot, blk):
        pv = _dot(p_scr[slot], vext_scr[pl.ds(pl.multiple_of(blk * tk, tk), tk), :])
        alpha = alpha_scr[slot]
        acc_scr[:, :LANES] = alpha * acc_scr[:, :LANES] + pv[:, :LANES]
        acc_scr[:, LANES:] = alpha * acc_scr[:, LANES:] + pv[:, LANES:]

    m_scr[...] = jnp.full(m_scr.shape, -jnp.inf, F32)
    acc_scr[...] = jnp.zeros(acc_scr.shape, F32)
    p_scr[1] = jnp.zeros(p_scr.shape[1:], BF16)
    alpha_scr[1] = jnp.ones(alpha_scr.shape[1:], F32)

    q_chunk = (i * tq + lax.broadcasted_iota(jnp.int32, (rows, 1), 0) % tq) // CHUNK
    k_chunk = (diag * tk + lax.broadcasted_iota(jnp.int32, (1, tk), 1)) // CHUNK
    s_scr[0] = jnp.where(k_chunk <= q_chunk, scores(diag), -jnp.inf)

    last = n_blocks - 1

    def body(pair, carry):
        t = 2 * pair
        s_scr[1] = scores(key_block(jnp.minimum(t + 1, last)))
        softmax(0)
        accumulate(1, key_block(jnp.maximum(t - 1, 0)))
        s_scr[0] = scores(key_block(jnp.minimum(t + 2, last)))
        softmax(1, jnp.where(t + 1 <= last, 0.0, -jnp.inf))
        accumulate(0, key_block(t))
        return carry

    n_pairs = (n_blocks + 1) // 2
    lax.fori_loop(0, n_pairs, body, 0)
    accumulate(1, key_block(jnp.minimum(2 * n_pairs - 1, last)))

    lam = _lambda_full(lam_ref, lam_init)
    acc = acc_scr[...]
    o_ref[...] = _diff_combine(acc[:, :LANES], acc[:, LANES:], tq, lam, g_ref[...], lam_init).astype(o_ref.dtype)


def attn_prompt(qb, kb, vb, lam_rows, sub_gain, lam_init, *, batch, seq, tq, tk):
    nq = seq // tq
    assert tk % tq == 0 and tq % CHUNK == 0, "one key block must hold all chunks of a query block"
    kern = functools.partial(_attn_prompt_kernel, tq=tq, tk=tk, lam_init=lam_init)
    return pl.pallas_call(
        kern,
        out_shape=jax.ShapeDtypeStruct((batch * seq, ATT_DIM), BF16),
        grid=(batch, N_HEADS, nq),
        in_specs=[pl.BlockSpec((4, HEAD_DIM), lambda b, h, i: (0, 0)),
                  pl.BlockSpec((1, LANES), lambda b, h, i: (0, 0)),
                  pl.BlockSpec((tq, LANES), lambda b, h, i: (b * nq + i, h)),
                  pl.BlockSpec((seq, LANES), lambda b, h, i: (b, h)),
                  pl.BlockSpec((seq, LANES), lambda b, h, i: (b, h))],
        out_specs=pl.BlockSpec((tq, LANES), lambda b, h, i: (b * nq + i, h)),
        scratch_shapes=[pltpu.VMEM((seq, 2 * LANES), BF16),
                        pltpu.VMEM((2, 2 * tq, tk), F32), pltpu.VMEM((2, 2 * tq, tk), BF16),
                        pltpu.VMEM((2 * tq, LANES), F32), pltpu.VMEM((2, 2 * tq, LANES), F32),
                        pltpu.VMEM((2 * tq, 2 * LANES), F32)],
        compiler_params=_params(("parallel", "parallel", "arbitrary")),
        name="attn_prompt",
    )(lam_rows, sub_gain, qb, kb, vb)


def _attn_sample_kernel(lam_ref, g_ref, q_ref, kn_ref, vn_ref, kc_ref, vc_ref, o_ref,
                        qq_scr, m_scr, l_scr, acc_scr, *, rows, lam_init):
    j = pl.program_id(1)
    hr = 2 * rows

    @pl.when(j == 0)
    def _():
        for h in range(N_HEADS):
            qq_scr[h * hr:(h + 1) * hr, :] = _split_components(q_ref[:, h * LANES:(h + 1) * LANES])
        m_scr[...] = jnp.full(m_scr.shape, -jnp.inf, F32)
        l_scr[...] = jnp.zeros(l_scr.shape, F32)
        acc_scr[...] = jnp.zeros(acc_scr.shape, F32)

    def update(k_ref, v_ref):
        s = jnp.concatenate([_dot_nt(qq_scr[h * hr:(h + 1) * hr, :], k_ref[:, h * LANES:(h + 1) * LANES])
                             for h in range(N_HEADS)], axis=0)
        m_old = m_scr[...]
        m_new = jnp.maximum(m_old, jnp.max(s, axis=-1, keepdims=True))
        alpha = jnp.exp(m_old - m_new)
        p = jnp.exp(s - m_new)
        l_scr[...] = alpha * l_scr[...] + jnp.sum(p, axis=-1, keepdims=True)
        m_scr[...] = m_new
        p = p.astype(BF16)
        for h in range(N_HEADS):
            hs = slice(h * hr, (h + 1) * hr)
            acc_scr[hs, :] = alpha[hs] * acc_scr[hs, :] + _dot(p[hs], v_ref[:, h * LANES:(h + 1) * LANES])

    update(kc_ref, vc_ref)

    @pl.when(j == pl.num_programs(1) - 1)
    def _():
        update(kn_ref, vn_ref)
        lam = _lambda_full(lam_ref, lam_init)
        for h in range(N_HEADS):
            hs = slice(h * hr, (h + 1) * hr)
            o_ref[:, h * LANES:(h + 1) * LANES] = _diff_combine(
                acc_scr[hs, :], l_scr[hs, :], rows, lam, g_ref[...], lam_init).astype(o_ref.dtype)


def attn_sample(qb, kb, vb, cache_k, cache_v, lam_rows, sub_gain, lam_init, *, layer, batch, rows, row0, tk):
    past = cache_k.shape[2]
    new_blk = pl.BlockSpec((rows, ATT_DIM), lambda b, j: (row0 // rows + b, 0))
    cache_blk = pl.BlockSpec((None, None, tk, ATT_DIM), lambda b, j: (layer, b, j, 0))
    kern = functools.partial(_attn_sample_kernel, rows=rows, lam_init=lam_init)
    return pl.pallas_call(
        kern,
        out_shape=jax.ShapeDtypeStruct((batch * rows, ATT_DIM), BF16),
        grid=(batch, past // tk),
        in_specs=[pl.BlockSpec((4, HEAD_DIM), lambda b, j: (0, 0)),
                  pl.BlockSpec((1, LANES), lambda b, j: (0, 0)),
                  new_blk, new_blk, new_blk, cache_blk, cache_blk],
        out_specs=pl.BlockSpec((rows, ATT_DIM), lambda b, j: (b, 0)),
        scratch_shapes=[pltpu.VMEM((N_HEADS * 2 * rows, LANES), BF16),
                        pltpu.VMEM((N_HEADS * 2 * rows, 1), F32), pltpu.VMEM((N_HEADS * 2 * rows, 1), F32),
                        pltpu.VMEM((N_HEADS * 2 * rows, LANES), F32)],
        compiler_params=_params(("parallel", "arbitrary")),
        name="attn_sample",
    )(lam_rows, sub_gain, qb, kb, vb, cache_k, cache_v)


def _ssd_kernel(xbc_ref, z_ref, dt_ref, conv0_ref, ssm0_ref, cw_ref, cb_ref, dtb_ref, alog_ref, dskip_ref, ng_ref,
                y_ref, ssm_ref, conv_ref, xpad_scr, *, lc):
    c = pl.program_id(1)
    pad0 = SUBLANES - (CONV_W - 1)

    @pl.when(c == 0)
    def _():
        xpad_scr[pad0:SUBLANES, :] = conv0_ref[...]
        ssm_ref[...] = ssm0_ref[...]

    xpad_scr[SUBLANES:, :] = xbc_ref[...]
    conv = cb_ref[...] + sum(xpad_scr[pad0 + j:pad0 + j + lc, :] * cw_ref[j:j + 1, :] for j in range(CONV_W))
    conv = _silu(conv)
    tail = xpad_scr[lc + pad0:lc + SUBLANES, :]
    xpad_scr[pad0:SUBLANES, :] = tail
    conv_ref[...] = tail

    dt = _softplus(dt_ref[...] + dtb_ref[...])
    da = dt * (-jnp.exp(alog_ref[...]))
    row = lax.broadcasted_iota(jnp.int32, (lc, lc), 0)
    col = lax.broadcasted_iota(jnp.int32, (lc, lc), 1)
    causal = row >= col
    a_cum = _dot_exact(causal.astype(F32), da)
    dt_t = _transpose_exact(dt)
    a_cum_t = _transpose_exact(a_cum)
    eye = (lax.broadcasted_iota(jnp.int32, (LANES, LANES), 0)
           == lax.broadcasted_iota(jnp.int32, (LANES, LANES), 1)).astype(BF16)

    for g in range(SSM_GROUPS):
        b_g = conv[:, D_INNER + g * D_STATE:D_INNER + (g + 1) * D_STATE].astype(BF16)
        c_off = D_INNER + SSM_GROUPS * D_STATE
        c_g = conv[:, c_off + g * D_STATE:c_off + (g + 1) * D_STATE].astype(BF16)
        cb = _dot_nt(c_g, b_g)
        ys = []
        for pair in range(HEADS_PER_GROUP // 2):
            lane0 = (g * HEADS_PER_GROUP + 2 * pair) * SSM_HEAD_DIM
            x_pair = conv[:, lane0:lane0 + LANES].astype(BF16)
            x_pair_t = _dot_nt(eye, x_pair)
            for sub in range(2):
                h = g * HEADS_PER_GROUP + 2 * pair + sub
                x_h = x_pair[:, sub * SSM_HEAD_DIM:(sub + 1) * SSM_HEAD_DIM]
                x_h_t = x_pair_t[sub * SSM_HEAD_DIM:(sub + 1) * SSM_HEAD_DIM, :]
                a_col = a_cum[:, h:h + 1]
                a_row = a_cum_t[h:h + 1, :]
                dt_row = dt_t[h:h + 1, :]
                a_last = a_cum_t[h:h + 1, lc - 1:lc]
                decay = jnp.exp(jnp.where(causal, a_col - a_row, -jnp.inf))
                w_in = (cb * decay * dt_row).astype(BF16)
                state = ssm_ref[h]
                y_h = _dot(w_in, x_h) + jnp.exp(a_col) * _dot_nt(c_g, state.astype(BF16))
                ys.append(y_h)
                w_state = (x_h_t * (dt_row * jnp.exp(a_last - a_row))).astype(BF16)
                ssm_ref[h] = jnp.exp(a_last) * state + _dot(w_state, b_g)
        lanes = slice(g * HEADS_PER_GROUP * SSM_HEAD_DIM, (g + 1) * HEADS_PER_GROUP * SSM_HEAD_DIM)
        y_g = jnp.concatenate(ys, axis=-1) + dskip_ref[:, lanes] * conv[:, lanes]
        y_g = y_g * _silu(z_ref[:, lanes])
        ms = jnp.mean(y_g * y_g, axis=-1, keepdims=True)
        y_ref[:, lanes] = (y_g * lax.rsqrt(ms + EPS) * ng_ref[:, lanes]).astype(y_ref.dtype)


def ssd(proj, conv0, ssm0, conv_w, conv_b, dt_bias, a_log, d_skip, norm_gain, *, batch, seq, row0, lc):
    nc = seq // lc
    tok = lambda width, col: pl.BlockSpec((lc, width), lambda b, c: (row0 // lc + b * nc + c, col // width))
    const = lambda r, w: pl.BlockSpec((r, w), lambda b, c: (0, 0))
    kern = functools.partial(_ssd_kernel, lc=lc)
    return pl.pallas_call(
        kern,
        out_shape=[jax.ShapeDtypeStruct((batch * seq, D_INNER), BF16),
                   jax.ShapeDtypeStruct((batch, SSM_HEADS, SSM_HEAD_DIM, D_STATE), F32),
                   jax.ShapeDtypeStruct((batch, CONV_W - 1, CONV_DIM), F32)],
        grid=(batch, nc),
        in_specs=[tok(CONV_DIM, COL_XBC), tok(D_INNER, COL_Z), tok(LANES, COL_DT),
                  pl.BlockSpec((None, CONV_W - 1, CONV_DIM), lambda b, c: (b, 0, 0)),
                  pl.BlockSpec((None, SSM_HEADS, SSM_HEAD_DIM, D_STATE), lambda b, c: (b, 0, 0, 0)),
                  const(CONV_W, CONV_DIM), const(1, CONV_DIM), const(1, LANES), const(1, LANES),
                  const(1, D_INNER), const(1, D_INNER)],
        out_specs=[pl.BlockSpec((lc, D_INNER), lambda b, c: (b * nc + c, 0)),
                   pl.BlockSpec((None, SSM_HEADS, SSM_HEAD_DIM, D_STATE), lambda b, c: (b, 0, 0, 0)),
                   pl.BlockSpec((None, CONV_W - 1, CONV_DIM), lambda b, c: (b, 0, 0))],
        scratch_shapes=[pltpu.VMEM((SUBLANES + lc, CONV_DIM), F32)],
        compiler_params=_params(("parallel", "arbitrary")),
        name="ssd",
    )(proj, proj, proj, conv0, ssm0, conv_w, conv_b, dt_bias, a_log, d_skip, norm_gain)


def _out_proj_kernel(x_ref, att_ref, y_ref, ga_ref, gm_ref, wa_ref, wm_ref, wo_ref, o_ref):
    merged = (_sigmoid(ga_ref[...]) * _dot(att_ref[...], wa_ref[...])
              + _sigmoid(gm_ref[...]) * _dot(y_ref[...], wm_ref[...]))
    o_ref[...] = x_ref[...] + _dot(merged.astype(BF16), wo_ref[...])


def out_proj(x, att, y, proj, wa, wm, wo, *, tm):
    t = x.shape[0]
    gate = lambda c: pl.BlockSpec((tm, D_MODEL), lambda i, c=c: (i, c // D_MODEL))
    const = lambda r: pl.BlockSpec((r, D_MODEL), lambda i: (0, 0))
    return pl.pallas_call(
        _out_proj_kernel,
        out_shape=jax.ShapeDtypeStruct((t, D_MODEL), F32),
        grid=(t // tm,),
        in_specs=[pl.BlockSpec((tm, D_MODEL), lambda i: (i, 0)),
                  pl.BlockSpec((tm, ATT_DIM), lambda i: (i, 0)),
                  pl.BlockSpec((tm, D_INNER), lambda i: (i, 0)),
                  gate(COL_GATE), gate(COL_GATE + D_MODEL),
                  const(ATT_DIM), const(D_INNER), const(D_MODEL)],
        out_specs=pl.BlockSpec((tm, D_MODEL), lambda i: (i, 0)),
        compiler_params=_params(("parallel",)),
        name="out_proj",
    )(x, att, y, proj, proj, wa, wm, wo)


def _top_ranked(s, k):
    vals = []
    rank = jnp.full(s.shape, float(k), F32)
    for i in range(k):
        m = jnp.max(s, axis=0, keepdims=True)
        hit = s == m
        vals.append(m)
        rank = jnp.where(hit, float(i), rank)
        s = jnp.where(hit, -jnp.inf, s)
    return vals, rank


def _peer_keys_kernel(x_ref, g_ref, wqt_ref, k1_ref, k2_ref, ht_ref, e1_ref, cnt_ref, e2_ref, r2_ref):
    x = x_ref[...]
    ms = jnp.mean(x * x, axis=-1, keepdims=True)
    h32 = x * lax.rsqrt(ms + EPS) * g_ref[...]
    h_t = h32.T.astype(BF16)
    ht_ref[...] = h_t
    q_t = _dot(wqt_ref[...], h_t)
    for hd in range(PK_HEADS):
        r0 = hd * 2 * PK_HALF
        s1 = _dot(k1_ref[hd], q_t[r0:r0 + PK_HALF, :].astype(BF16))
        s2 = _dot(k2_ref[hd], q_t[r0 + PK_HALF:r0 + 2 * PK_HALF, :].astype(BF16))
        v1, r1 = _top_ranked(s1, PK_TOPK)
        v2, r2 = _top_ranked(s2, PK_TOPK)
        widths = [PK_TOPK // (a + 1) for a in range(PK_TOPK)]
        pad_rows = -sum(widths) % SUBLANES
        cand = jnp.concatenate([v1[a] + jnp.concatenate(v2[:widths[a]], axis=0) for a in range(PK_TOPK)]
                               + [jnp.full((pad_rows, s1.shape[1]), -jnp.inf, F32)], axis=0)
        tau = _top_ranked(cand, PK_TOPK)[0][-1]
        top = v1[0] + v2[0]
        chosen = cand >= tau
        z = jnp.sum(jnp.where(chosen, jnp.exp(cand - top), 0.0), axis=0, keepdims=True)
        cnt = jnp.zeros(s1.shape, F32)
        row0 = 0
        for a in range(PK_TOPK):
            n_a = jnp.sum(chosen[row0:row0 + widths[a]].astype(F32), axis=0, keepdims=True)
            cnt = jnp.where(r1 == float(a), n_a, cnt)
            row0 += widths[a]
        e1_ref[hd] = jnp.exp(s1 - v1[0]) / z
        cnt_ref[hd] = cnt
        e2_ref[hd] = jnp.exp(s2 - v2[0])
        r2_ref[hd] = r2


def peer_keys(x, gain, wq_t, k1, k2, *, tt):
    t = x.shape[0]
    per_head = pl.BlockSpec((PK_HEADS, N_KEYS, tt), lambda i: (0, 0, i))
    table = jax.ShapeDtypeStruct((PK_HEADS, N_KEYS, t), F32)
    return pl.pallas_call(
        _peer_keys_kernel,
        out_shape=[jax.ShapeDtypeStruct((D_MODEL, t), BF16), table, table, table, table],
        grid=(t // tt,),
        in_specs=[pl.BlockSpec((tt, D_MODEL), lambda i: (i, 0)),
                  pl.BlockSpec((1, D_MODEL), lambda i: (0, 0)),
                  pl.BlockSpec((PK_HEADS * 2 * PK_HALF, D_MODEL), lambda i: (0, 0)),
                  pl.BlockSpec((PK_HEADS, N_KEYS, PK_HALF), lambda i: (0, 0, 0)),
                  pl.BlockSpec((PK_HEADS, N_KEYS, PK_HALF), lambda i: (0, 0, 0))],
        out_specs=[pl.BlockSpec((D_MODEL, tt), lambda i: (0, i)), per_head, per_head, per_head, per_head],
        compiler_params=_params(("parallel",)),
        name="peer_keys",
    )(x, gain, wq_t, k1, k2)


def _gelu(x):
    return 0.5 * x * (1.0 + lax.erf(x * (2.0 ** -0.5)))


def _peer_mix_kernel(x_ref, h_ref, u_ref, vt_ref, e1_ref, cnt_ref, e2_ref, r2_ref, o_ref,
                     acc_scr, raw_scr, coef_scr, *, eb):
    e = pl.program_id(1)
    tt = acc_scr.shape[1]
    n_sub = eb // PEER_SUB
    rows_per_sub = PEER_SUB // N_KEYS
    pack = 2 * SUBLANES

    @pl.when(e == 0)
    def _():
        acc_scr[...] = jnp.zeros(acc_scr.shape, F32)

    def pre_activations(j):
        start = pl.multiple_of(j * PEER_SUB, PEER_SUB)
        return _dot(u_ref[pl.ds(start, PEER_SUB), :], h_ref[...])

    def mix(slot, j):
        for r in range(rows_per_sub):
            i1 = (e * n_sub + j) * rows_per_sub + r
            cnt_rows = [cnt_ref[hd, pl.ds(i1, 1), :] for hd in range(PK_HEADS)]
            e1_rows = [e1_ref[hd, pl.ds(i1, 1), :] for hd in range(PK_HEADS)]
            for lt in range(tt // LANES):
                lanes = slice(lt * LANES, (lt + 1) * LANES)
                cnt_b = [jnp.broadcast_to(row[:, lanes], (SUBLANES, LANES)) for row in cnt_rows]
                e1_b = [jnp.broadcast_to(row[:, lanes], (SUBLANES, LANES)) for row in e1_rows]
                for c in range(N_KEYS // pack):
                    tiles = []
                    for half in range(2):
                        k0 = c * pack + half * SUBLANES
                        keys = slice(k0, k0 + SUBLANES)
                        gate = jnp.zeros((SUBLANES, LANES), F32)
                        for hd in range(PK_HEADS):
                            keep = r2_ref[hd, keys, lanes] < cnt_b[hd]
                            gate = gate + jnp.where(keep, e2_ref[hd, keys, lanes] * e1_b[hd], 0.0)
                        tiles.append(gate * _gelu(raw_scr[slot, r * N_KEYS + k0:r * N_KEYS + k0 + SUBLANES, lanes]))
                    rows = slice(r * N_KEYS + c * pack, r * N_KEYS + (c + 1) * pack)
                    coef_scr[slot, rows, lanes] = jnp.concatenate(tiles, axis=0).astype(BF16)

    def accumulate(slot, j):
        acc_scr[...] += _dot(vt_ref[j], coef_scr[slot])

    raw_scr[0] = pre_activations(0)
    coef_scr[1] = jnp.zeros(coef_scr.shape[1:], BF16)

    def body(pair, carry):
        j = 2 * pair
        raw_scr[1] = pre_activations(j + 1)
        mix(0, j)
        accumulate(1, jnp.maximum(j - 1, 0))
        raw_scr[0] = pre_activations(jnp.minimum(j + 2, n_sub - 1))
        mix(1, j + 1)
        accumulate(0, j)
        return carry

    lax.fori_loop(0, n_sub // 2, body, 0)
    accumulate(1, n_sub - 1)

    @pl.when(e == pl.num_programs(1) - 1)
    def _():
        o_ref[...] = x_ref[...] + acc_scr[...].T


def peer_mix(x, h, u, v_sub, e1, cnt, e2, r2, *, tt, eb):
    t = x.shape[0]
    assert (eb // PEER_SUB) % 2 == 0
    per_head = pl.BlockSpec((PK_HEADS, N_KEYS, tt), lambda i, e: (0, 0, i))
    kern = functools.partial(_peer_mix_kernel, eb=eb)
    return pl.pallas_call(
        kern,
        out_shape=jax.ShapeDtypeStruct((t, D_MODEL), F32),
        grid=(t // tt, N_EXPERTS // eb),
        in_specs=[pl.BlockSpec((tt, D_MODEL), lambda i, e: (i, 0)),
                  pl.BlockSpec((D_MODEL, tt), lambda i, e: (0, i)),
                  pl.BlockSpec((eb, D_MODEL), lambda i, e: (e, 0)),
                  pl.BlockSpec((eb // PEER_SUB, D_MODEL, PEER_SUB), lambda i, e: (e, 0, 0)),
                  per_head, per_head, per_head, per_head],
        out_specs=pl.BlockSpec((tt, D_MODEL), lambda i, e: (i, 0)),
        scratch_shapes=[pltpu.VMEM((D_MODEL, tt), F32), pltpu.VMEM((2, PEER_SUB, tt), F32),
                        pltpu.VMEM((2, PEER_SUB, tt), BF16)],
        compiler_params=_params(("parallel", "arbitrary")),
        name="peer_mix",
    )(x, h, u, v_sub, e1, cnt, e2, r2)


def _rope_tables(pos):
    half = HEAD_DIM // 2
    inv_freq = ROPE_THETA ** (-jnp.arange(half, dtype=F32) / half)
    ang = pos[:, None] * inv_freq[None, :]
    cos, sin = jnp.cos(ang), jnp.sin(ang)
    reps = LANES // HEAD_DIM
    return jnp.tile(jnp.concatenate([cos, cos], -1), (1, reps)), jnp.tile(jnp.concatenate([-sin, sin], -1), (1, reps))


def _lambda_init(layer):
    return 0.8 - 0.6 * math.exp(-0.3 * layer)


def _pad_lanes(v, width=LANES):
    return jnp.pad(v, (0, width - v.shape[0])).reshape(1, width)


def kernel(x_prompt, x_sample, cache_k, cache_v, state_ssm, state_conv, norm_mix, w_in, q_norm, k_norm, lam_q1, lam_k1, lam_q2, lam_k2, attn_subln, conv_w, conv_b, dt_bias, a_log, d_skip, ssm_norm, w_proj_attn, w_proj_ssm, w_out, norm_ffn, peer_wq, peer_k1, peer_k2, peer_u, peer_v):
    bp, lp, _ = x_prompt.shape
    bs, ls, _ = x_sample.shape
    depth = w_in.shape[0]
    n_past = cache_k.shape[2]
    tp, ts = bp * lp, bs * ls
    x = jnp.concatenate([x_prompt.reshape(tp, D_MODEL), x_sample.reshape(ts, D_MODEL)], axis=0)
    t_all = tp + ts
    tm = math.gcd(t_all, 1024)
    tt = math.gcd(t_all, 256)

    cos_p, sin_p = _rope_tables(jnp.tile(jnp.arange(lp, dtype=F32), bp))
    cos_s, sin_s = _rope_tables(jnp.tile(n_past + jnp.arange(ls, dtype=F32), bs))
    cache_k = cache_k.reshape(depth, bs, n_past, ATT_DIM).astype(BF16)
    cache_v = cache_v.reshape(depth, bs, n_past, ATT_DIM).astype(BF16)
    conv0_p = jnp.zeros((bp, CONV_W - 1, CONV_DIM), F32)
    ssm0_p = jnp.zeros((bp, SSM_HEADS, SSM_HEAD_DIM, D_STATE), F32)

    outs = {name: [] for name in ("kp", "vp", "sp", "cp", "ks", "vs", "ss", "cs")}
    for i in range(depth):
        li = _lambda_init(i)
        w_i = w_in[i].astype(BF16)
        w_qkv = w_i[:, :3 * ATT_DIM]
        w_rest = jnp.zeros((D_MODEL, PROJ_COLS), BF16)
        for col, piece in ((COL_Z, w_i[:, 3072:5120]), (COL_GATE, w_i[:, 9248:]), (COL_XBC, w_i[:, 5120:9216]),
                           (COL_DT, w_i[:, 9216:9248])):
            w_rest = lax.dynamic_update_slice(w_rest, piece, (0, col))
        gain_mix = norm_mix[i].reshape(1, D_MODEL)
        proj = in_proj(x, gain_mix, w_rest, tm=tm)

        q_gain = jnp.tile(q_norm[i], LANES // HEAD_DIM).reshape(1, LANES)
        k_gain = jnp.tile(k_norm[i], LANES // HEAD_DIM).reshape(1, LANES)
        qkv_w = (gain_mix, w_qkv)
        qb_p, k_p, kb_p, v_p, vb_p = qkv_proj(x, *qkv_w, cos_p, sin_p, q_gain, k_gain, row0=0, rows=tp,
                                              tm=math.gcd(tp, 512))
        qb_s, k_s, kb_s, v_s, vb_s = qkv_proj(x, *qkv_w, cos_s, sin_s, q_gain, k_gain, row0=tp, rows=ts,
                                              tm=math.gcd(math.gcd(tp, ts), 512))

        lam_rows = jnp.stack([lam_q1[i], lam_k1[i], lam_q2[i], lam_k2[i]])
        sub_gain = attn_subln[i].reshape(1, LANES)
        att_p = attn_prompt(qb_p, kb_p, vb_p, lam_rows, sub_gain, li, batch=bp, seq=lp,
                            tq=min(ATT_TQ, lp), tk=min(ATT_TK, lp))
        att_s = attn_sample(qb_s, kb_s, vb_s, cache_k, cache_v, lam_rows, sub_gain, li,
                            layer=i, batch=bs, rows=ls, row0=0, tk=min(ATT_TKS, n_past))
        att = jnp.concatenate([att_p, att_s], axis=0)

        ssd_w = (conv_w[i], conv_b[i].reshape(1, CONV_DIM), _pad_lanes(dt_bias[i]), _pad_lanes(a_log[i]),
                 jnp.repeat(d_skip[i], SSM_HEAD_DIM).reshape(1, D_INNER), ssm_norm[i].reshape(1, D_INNER))
        y_p, ssm_p, conv_p = ssd(proj, conv0_p, ssm0_p, *ssd_w, batch=bp, seq=lp, row0=0, lc=min(CHUNK, lp))
        y_s, ssm_s, conv_s = ssd(proj, state_conv[i], state_ssm[i], *ssd_w, batch=bs, seq=ls, row0=tp,
                                 lc=min(CHUNK, ls))
        y = jnp.concatenate([y_p, y_s], axis=0)

        x = out_proj(x, att, y, proj, w_proj_attn[i].astype(BF16), w_proj_ssm[i].astype(BF16),
                     w_out[i].astype(BF16), tm=min(tm, 512))

        h2, *tables = peer_keys(x, norm_ffn[i].reshape(1, D_MODEL), peer_wq[i].T.astype(BF16),
                                peer_k1[i].astype(BF16), peer_k2[i].astype(BF16), tt=tt)
        v_sub = peer_v[i].astype(BF16).reshape(N_EXPERTS // PEER_SUB, PEER_SUB, D_MODEL).transpose(0, 2, 1)
        x = peer_mix(x, h2, peer_u[i].astype(BF16), v_sub, *tables, tt=tt, eb=PEER_EB)

        outs["kp"].append(k_p.reshape(bp, lp, N_HEADS, 2, HEAD_DIM))
        outs["vp"].append(v_p.reshape(bp, lp, N_HEADS, 2 * HEAD_DIM))
        outs["sp"].append(ssm_p)
        outs["cp"].append(conv_p)
        outs["ks"].append(k_s.reshape(bs, ls, N_HEADS, 2, HEAD_DIM))
        outs["vs"].append(v_s.reshape(bs, ls, N_HEADS, 2 * HEAD_DIM))
        outs["ss"].append(ssm_s)
        outs["cs"].append(conv_s)

    stack = lambda name: jnp.stack(outs[name])
    return (x[:tp].reshape(bp, lp, D_MODEL), x[tp:].reshape(bs, ls, D_MODEL),
            stack("kp"), stack("vp"), stack("sp"), stack("cp"),
            stack("ks"), stack("vs"), stack("ss"), stack("cs"))
```

```python
import functools
import math

import jax
import jax.numpy as jnp
from jax import lax
from jax.experimental import pallas as pl
from jax.experimental.pallas import tpu as pltpu

F32 = jnp.float32
BF16 = jnp.bfloat16

D_MODEL = 1024
DEPTH = 4
CHUNK = 64
N_HEADS = 8
HEAD_DIM = 64
ATT_DIM = N_HEADS * 2 * HEAD_DIM
ROPE_THETA = 10000.0
D_INNER = 2048
SSM_HEAD_DIM = 64
SSM_HEADS = D_INNER // SSM_HEAD_DIM
SSM_GROUPS = 8
HEADS_PER_GROUP = SSM_HEADS // SSM_GROUPS
D_STATE = 128
CONV_W = 4
CONV_DIM = D_INNER + 2 * SSM_GROUPS * D_STATE
GATE_DIM = 2 * D_MODEL
PK_HEADS = 8
N_KEYS = 128
N_EXPERTS = N_KEYS * N_KEYS
PK_TOPK = 16
PK_HALF = 128
EPS = 1e-6

LANES = 128
SUBLANES = 8
VMEM_LIMIT = 56 * 1024 * 1024

COL_Z = 0
COL_GATE = 2048
COL_XBC = 4096
COL_DT = 8192
PROJ_COLS = 8320
PROJ_TN = 1664
ATT_TQ = 512
ATT_TK = 512
ATT_TKS = 1024
ATT_HPS = 2
PEER_SUB = 256
PEER_EB = 4096

HIGHEST = lax.Precision.HIGHEST
NT_DIMS = (((1,), (1,)), ((), ()))


def _params(sem):
    return pltpu.CompilerParams(dimension_semantics=sem, vmem_limit_bytes=VMEM_LIMIT)


def _dot(a, b):
    return jnp.dot(a, b, preferred_element_type=F32)


def _dot_nt(a, b):
    return lax.dot_general(a, b, NT_DIMS, preferred_element_type=F32)


def _dot_exact(a, b):
    return jnp.dot(a, b, preferred_element_type=F32, precision=HIGHEST)


def _dot_split(a, b):
    head = a.astype(BF16)
    rest = (a - head.astype(F32)).astype(BF16)
    return _dot(head, b) + _dot(rest, b)


def _transpose_exact(x):
    c = x.shape[1]
    eye = (lax.broadcasted_iota(jnp.int32, (c, c), 0) == lax.broadcasted_iota(jnp.int32, (c, c), 1)).astype(F32)
    return lax.dot_general(eye, x, NT_DIMS, preferred_element_type=F32, precision=HIGHEST)


def _sigmoid(x):
    return 1.0 / (1.0 + jnp.exp(-x))


def _silu(x):
    return x * _sigmoid(x)


def _softplus(x):
    return jnp.maximum(x, 0.0) + jnp.log(1.0 + jnp.exp(-jnp.abs(x)))


def _in_proj_kernel(x_ref, g_ref, w_ref, o_ref, h_scr):
    @pl.when(pl.program_id(1) == 0)
    def _():
        x = x_ref[...]
        ms = jnp.mean(x * x, axis=-1, keepdims=True)
        h_scr[...] = (x * lax.rsqrt(ms + EPS) * g_ref[...]).astype(BF16)

    o_ref[...] = _dot(h_scr[...], w_ref[...])


def in_proj(x, gain, w, *, tm):
    t = x.shape[0]
    n = w.shape[1]
    return pl.pallas_call(
        _in_proj_kernel,
        out_shape=jax.ShapeDtypeStruct((t, n), F32),
        grid=(t // tm, n // PROJ_TN),
        in_specs=[pl.BlockSpec((tm, D_MODEL), lambda i, j: (i, 0)),
                  pl.BlockSpec((1, D_MODEL), lambda i, j: (0, 0)),
                  pl.BlockSpec((D_MODEL, PROJ_TN), lambda i, j: (0, j))],
        out_specs=pl.BlockSpec((tm, PROJ_TN), lambda i, j: (i, j)),
        scratch_shapes=[pltpu.VMEM((tm, D_MODEL), BF16)],
        compiler_params=_params(("parallel", "arbitrary")),
        name="in_proj",
    )(x, gain, w)


def _qkv_proj_kernel(x_ref, g_ref, w_ref, cos_ref, sin_ref, qg_ref, kg_ref,
                     qb_ref, ko_ref, kb_ref, vo_ref, vb_ref):
    x = x_ref[...]
    ms = jnp.mean(x * x, axis=-1, keepdims=True)
    h = (x * lax.rsqrt(ms + EPS) * g_ref[...]).astype(BF16)
    qkv = _dot(h, w_ref[...])
    cos = cos_ref[...]
    sin = sin_ref[...]
    row = lax.broadcasted_iota(jnp.int32, (LANES, LANES), 0) // HEAD_DIM
    col = lax.broadcasted_iota(jnp.int32, (LANES, LANES), 1) // HEAD_DIM
    group_mean = jnp.where(row == col, 1.0 / HEAD_DIM, 0.0).astype(BF16)
    lane = lax.broadcasted_iota(jnp.int32, (1, LANES), 1)
    low_half = (lane % HEAD_DIM) < (HEAD_DIM // 2)

    def norm_rope(y, gain):
        ms = _dot_split(y * y, group_mean)
        y = y * lax.rsqrt(ms + EPS) * gain
        partner = jnp.where(low_half, pltpu.roll(y, LANES - HEAD_DIM // 2, 1), pltpu.roll(y, HEAD_DIM // 2, 1))
        return y * cos + partner * sin

    for h_i in range(N_HEADS):
        sl = slice(h_i * LANES, (h_i + 1) * LANES)
        q = norm_rope(qkv[:, h_i * LANES:(h_i + 1) * LANES], qg_ref[...]) * (HEAD_DIM ** -0.5)
        qb_ref[:, sl] = q.astype(BF16)
        k = norm_rope(qkv[:, ATT_DIM + h_i * LANES:ATT_DIM + (h_i + 1) * LANES], kg_ref[...])
        ko_ref[:, sl] = k
        kb_ref[:, sl] = k.astype(BF16)
    v = qkv[:, 2 * ATT_DIM:]
    vo_ref[...] = v
    vb_ref[...] = v.astype(BF16)


def qkv_proj(x, gain, w_qkv, cos, sin, q_gain, k_gain, *, row0, rows, tm):
    t = rows
    tile0 = row0 // tm
    tab = pl.BlockSpec((tm, LANES), lambda i: (i, 0))
    vec = pl.BlockSpec((1, LANES), lambda i: (0, 0))
    out = pl.BlockSpec((tm, ATT_DIM), lambda i: (i, 0))
    return pl.pallas_call(
        _qkv_proj_kernel,
        out_shape=[jax.ShapeDtypeStruct((t, ATT_DIM), BF16),
                   jax.ShapeDtypeStruct((t, ATT_DIM), F32),
                   jax.ShapeDtypeStruct((t, ATT_DIM), BF16),
                   jax.ShapeDtypeStruct((t, ATT_DIM), F32),
                   jax.ShapeDtypeStruct((t, ATT_DIM), BF16)],
        grid=(t // tm,),
        in_specs=[pl.BlockSpec((tm, D_MODEL), lambda i: (tile0 + i, 0)),
                  pl.BlockSpec((1, D_MODEL), lambda i: (0, 0)),
                  pl.BlockSpec((D_MODEL, 3 * ATT_DIM), lambda i: (0, 0)),
                  tab, tab, vec, vec],
        out_specs=[out, out, out, out, out],
        compiler_params=_params(("parallel",)),
        name="qkv_proj",
    )(x, gain, w_qkv, cos, sin, q_gain, k_gain)


def _lambda_full(lam_ref, lam_init):
    a = jnp.sum(lam_ref[0:1, :] * lam_ref[1:2, :], axis=-1, keepdims=True)
    b = jnp.sum(lam_ref[2:3, :] * lam_ref[3:4, :], axis=-1, keepdims=True)
    return jnp.exp(a) - jnp.exp(b) + lam_init


def _split_components(q):
    lane = lax.broadcasted_iota(jnp.int32, (1, LANES), 1)
    zero = jnp.zeros_like(q)
    return jnp.concatenate([jnp.where(lane < HEAD_DIM, q, zero), jnp.where(lane >= HEAD_DIM, q, zero)], axis=0)


def _flash_update(s, v, m_ref, l_ref, acc_ref):
    m_old = m_ref[...]
    m_new = jnp.maximum(m_old, jnp.max(s, axis=-1, keepdims=True))
    alpha = jnp.exp(m_old - m_new)
    p = jnp.exp(s - m_new)
    l_ref[...] = alpha * l_ref[...] + jnp.sum(p, axis=-1, keepdims=True)
    acc_ref[...] = alpha * acc_ref[...] + _dot(p.astype(BF16), v)
    m_ref[...] = m_new


def _diff_combine(acc, l, rows, lam, gain, lam_init):
    o1 = acc[:rows] / l[:rows]
    o2 = acc[rows:] / l[rows:]
    o = o1 - lam * o2
    ms = jnp.mean(o * o, axis=-1, keepdims=True)
    return o * lax.rsqrt(ms + EPS) * gain * (1.0 - lam_init)


def _attn_prompt_kernel(lam_ref, g_ref, q_ref, k_ref, v_ref, o_ref,
                        vext_scr, s_scr, p_scr, m_scr, alpha_scr, acc_scr, *, tq, tk, hps, lam_init):
    i = pl.program_id(2)
    rows = 2 * tq
    heads = range(hps)

    @pl.when(i == 0)
    def _():
        for hh in heads:
            vext_scr[hh, :, :LANES] = v_ref[:, hh * LANES:(hh + 1) * LANES]
            vext_scr[hh, :, LANES:] = jnp.ones((vext_scr.shape[1], LANES), BF16)

    qq = [_split_components(q_ref[:, hh * LANES:(hh + 1) * LANES]) for hh in heads]
    diag = (i * tq) // tk
    n_blocks = diag + 1

    def key_block(t):
        return jnp.where(t == 0, diag, t - 1)

    def scores(hh, blk):
        return _dot_nt(qq[hh], k_ref[pl.ds(pl.multiple_of(blk * tk, tk), tk), hh * LANES:(hh + 1) * LANES])

    def softmax(hh, slot, bias=None):
        s = s_scr[hh, slot]
        if bias is not None:
            s = s + bias
        m_old = m_scr[hh]
        m_new = jnp.maximum(m_old, jnp.max(s, axis=-1, keepdims=True))
        alpha_scr[hh, slot] = jnp.exp(m_old - m_new)
        for c in range(tk // LANES):
            sl = slice(c * LANES, (c + 1) * LANES)
            p_scr[hh, slot, :, sl] = jnp.exp(s[:, sl] - m_new).astype(BF16)
        m_scr[hh] = m_new

    def accumulate(hh, slot, blk):
        pv = _dot(p_scr[hh, slot], vext_scr[hh, pl.ds(pl.multiple_of(blk * tk, tk), tk), :])
        alpha = alpha_scr[hh, slot]
        acc_scr[hh, :, :LANES] = alpha * acc_scr[hh, :, :LANES] + pv[:, :LANES]
        acc_scr[hh, :, LANES:] = alpha * acc_scr[hh, :, LANES:] + pv[:, LANES:]

    m_scr[...] = jnp.full(m_scr.shape, -jnp.inf, F32)
    acc_scr[...] = jnp.zeros(acc_scr.shape, F32)
    q_chunk = (i * tq + lax.broadcasted_iota(jnp.int32, (rows, 1), 0) % tq) // CHUNK
    k_chunk = (diag * tk + lax.broadcasted_iota(jnp.int32, (1, tk), 1)) // CHUNK
    for hh in heads:
        p_scr[hh, 1] = jnp.zeros(p_scr.shape[2:], BF16)
        alpha_scr[hh, 1] = jnp.ones(alpha_scr.shape[2:], F32)
        s_scr[hh, 0] = jnp.where(k_chunk <= q_chunk, scores(hh, diag), -jnp.inf)

    last = n_blocks - 1

    def body(pair, carry):
        t = 2 * pair
        for hh in heads:
            s_scr[hh, 1] = scores(hh, key_block(jnp.minimum(t + 1, last)))
            softmax(hh, 0)
            accumulate(hh, 1, key_block(jnp.maximum(t - 1, 0)))
        for hh in heads:
            s_scr[hh, 0] = scores(hh, key_block(jnp.minimum(t + 2, last)))
            softmax(hh, 1, jnp.where(t + 1 <= last, 0.0, -jnp.inf))
            accumulate(hh, 0, key_block(t))
        return carry

    n_pairs = (n_blocks + 1) // 2
    lax.fori_loop(0, n_pairs, body, 0)
    lam = _lambda_full(lam_ref, lam_init)
    for hh in heads:
        accumulate(hh, 1, key_block(jnp.minimum(2 * n_pairs - 1, last)))
        acc = acc_scr[hh]
        o_ref[:, hh * LANES:(hh + 1) * LANES] = _diff_combine(
            acc[:, :LANES], acc[:, LANES:], tq, lam, g_ref[...], lam_init).astype(o_ref.dtype)


def attn_prompt(qb, kb, vb, lam_rows, sub_gain, lam_init, *, batch, seq, tq, tk, hps):
    nq = seq // tq
    assert tk % tq == 0 and tq % CHUNK == 0, "one key block must hold all chunks of a query block"
    assert N_HEADS % hps == 0
    kern = functools.partial(_attn_prompt_kernel, tq=tq, tk=tk, hps=hps, lam_init=lam_init)
    width = hps * LANES
    return pl.pallas_call(
        kern,
        out_shape=jax.ShapeDtypeStruct((batch * seq, ATT_DIM), BF16),
        grid=(batch, N_HEADS // hps, nq),
        in_specs=[pl.BlockSpec((4, HEAD_DIM), lambda b, h, i: (0, 0)),
                  pl.BlockSpec((1, LANES), lambda b, h, i: (0, 0)),
                  pl.BlockSpec((tq, width), lambda b, h, i: (b * nq + i, h)),
                  pl.BlockSpec((seq, width), lambda b, h, i: (b, h)),
                  pl.BlockSpec((seq, width), lambda b, h, i: (b, h))],
        out_specs=pl.BlockSpec((tq, width), lambda b, h, i: (b * nq + i, h)),
        scratch_shapes=[pltpu.VMEM((hps, seq, 2 * LANES), BF16),
                        pltpu.VMEM((hps, 2, 2 * tq, tk), F32), pltpu.VMEM((hps, 2, 2 * tq, tk), BF16),
                        pltpu.VMEM((hps, 2 * tq, LANES), F32), pltpu.VMEM((hps, 2, 2 * tq, LANES), F32),
                        pltpu.VMEM((hps, 2 * tq, 2 * LANES), F32)],
        compiler_params=_params(("parallel", "parallel", "arbitrary")),
        name="attn_prompt",
    )(lam_rows, sub_gain, qb, kb, vb)


def _attn_sample_kernel(lam_ref, g_ref, q_ref, kn_ref, vn_ref, kc_ref, vc_ref, o_ref,
                        qq_scr, m_scr, l_scr, acc_scr, *, rows, lam_init):
    j = pl.program_id(1)
    hr = 2 * rows

    @pl.when(j == 0)
    def _():
        for h in range(N_HEADS):
            qq_scr[h * hr:(h + 1) * hr, :] = _split_components(q_ref[:, h * LANES:(h + 1) * LANES])
        m_scr[...] = jnp.full(m_scr.shape, -jnp.inf, F32)
        l_scr[...] = jnp.zeros(l_scr.shape, F32)
        acc_scr[...] = jnp.zeros(acc_scr.shape, F32)

    def update(k_ref, v_ref):
        s = jnp.concatenate([_dot_nt(qq_scr[h * hr:(h + 1) * hr, :], k_ref[:, h * LANES:(h + 1) * LANES].astype(BF16))
                             for h in range(N_HEADS)], axis=0)
        m_old = m_scr[...]
        m_new = jnp.maximum(m_old, jnp.max(s, axis=-1, keepdims=True))
        alpha = jnp.exp(m_old - m_new)
        p = jnp.exp(s - m_new)
        l_scr[...] = alpha * l_scr[...] + jnp.sum(p, axis=-1, keepdims=True)
        m_scr[...] = m_new
        p = p.astype(BF16)
        for h in range(N_HEADS):
            hs = slice(h * hr, (h + 1) * hr)
            acc_scr[hs, :] = alpha[hs] * acc_scr[hs, :] + _dot(p[hs], v_ref[:, h * LANES:(h + 1) * LANES].astype(BF16))

    update(kc_ref, vc_ref)

    @pl.when(j == pl.num_programs(1) - 1)
    def _():
        update(kn_ref, vn_ref)
        lam = _lambda_full(lam_ref, lam_init)
        for h in range(N_HEADS):
            hs = slice(h * hr, (h + 1) * hr)
            o_ref[:, h * LANES:(h + 1) * LANES] = _diff_combine(
                acc_scr[hs, :], l_scr[hs, :], rows, lam, g_ref[...], lam_init).astype(o_ref.dtype)


def attn_sample(qb, kb, vb, cache_k, cache_v, lam_rows, sub_gain, lam_init, *, layer, batch, rows, row0, tk):
    past = cache_k.shape[2]
    new_blk = pl.BlockSpec((rows, ATT_DIM), lambda b, j: (row0 // rows + b, 0))
    cache_blk = pl.BlockSpec((None, None, tk, ATT_DIM), lambda b, j: (layer, b, j, 0))
    kern = functools.partial(_attn_sample_kernel, rows=rows, lam_init=lam_init)
    return pl.pallas_call(
        kern,
        out_shape=jax.ShapeDtypeStruct((batch * rows, ATT_DIM), BF16),
        grid=(batch, past // tk),
        in_specs=[pl.BlockSpec((4, HEAD_DIM), lambda b, j: (0, 0)),
                  pl.BlockSpec((1, LANES), lambda b, j: (0, 0)),
                  new_blk, new_blk, new_blk, cache_blk, cache_blk],
        out_specs=pl.BlockSpec((rows, ATT_DIM), lambda b, j: (b, 0)),
        scratch_shapes=[pltpu.VMEM((N_HEADS * 2 * rows, LANES), BF16),
                        pltpu.VMEM((N_HEADS * 2 * rows, 1), F32), pltpu.VMEM((N_HEADS * 2 * rows, 1), F32),
                        pltpu.VMEM((N_HEADS * 2 * rows, LANES), F32)],
        compiler_params=_params(("parallel", "arbitrary")),
        name="attn_sample",
    )(lam_rows, sub_gain, qb, kb, vb, cache_k, cache_v)


def _ssd_kernel(xbc_ref, z_ref, dt_ref, conv0_ref, ssm0_ref, cw_ref, cb_ref, dtb_ref, alog_ref, dskip_ref, ng_ref,
                y_ref, ssm_ref, conv_ref, xpad_scr, *, lc):
    c = pl.program_id(1)
    pad0 = SUBLANES - (CONV_W - 1)

    @pl.when(c == 0)
    def _():
        xpad_scr[pad0:SUBLANES, :] = conv0_ref[...]
        ssm_ref[...] = ssm0_ref[...]

    xpad_scr[SUBLANES:, :] = xbc_ref[...]
    conv = cb_ref[...] + sum(xpad_scr[pad0 + j:pad0 + j + lc, :] * cw_ref[j:j + 1, :] for j in range(CONV_W))
    conv = _silu(conv)
    tail = xpad_scr[lc + pad0:lc + SUBLANES, :]
    xpad_scr[pad0:SUBLANES, :] = tail
    conv_ref[...] = tail

    dt = _softplus(dt_ref[...] + dtb_ref[...])
    da = dt * (-jnp.exp(alog_ref[...]))
    row = lax.broadcasted_iota(jnp.int32, (lc, lc), 0)
    col = lax.broadcasted_iota(jnp.int32, (lc, lc), 1)
    causal = row >= col
    a_cum = _dot_exact(causal.astype(F32), da)
    dt_t = _transpose_exact(dt)
    a_cum_t = _transpose_exact(a_cum)
    eye = (lax.broadcasted_iota(jnp.int32, (LANES, LANES), 0)
           == lax.broadcasted_iota(jnp.int32, (LANES, LANES), 1)).astype(BF16)

    for g in range(SSM_GROUPS):
        b_g = conv[:, D_INNER + g * D_STATE:D_INNER + (g + 1) * D_STATE].astype(BF16)
        c_off = D_INNER + SSM_GROUPS * D_STATE
        c_g = conv[:, c_off + g * D_STATE:c_off + (g + 1) * D_STATE].astype(BF16)
        cb = _dot_nt(c_g, b_g)
        ys = []
        for pair in range(HEADS_PER_GROUP // 2):
            lane0 = (g * HEADS_PER_GROUP + 2 * pair) * SSM_HEAD_DIM
            x_pair = conv[:, lane0:lane0 + LANES].astype(BF16)
            x_pair_t = _dot_nt(eye, x_pair)
            for sub in range(2):
                h = g * HEADS_PER_GROUP + 2 * pair + sub
                x_h = x_pair[:, sub * SSM_HEAD_DIM:(sub + 1) * SSM_HEAD_DIM]
                x_h_t = x_pair_t[sub * SSM_HEAD_DIM:(sub + 1) * SSM_HEAD_DIM, :]
                a_col = a_cum[:, h:h + 1]
                a_row = a_cum_t[h:h + 1, :]
                dt_row = dt_t[h:h + 1, :]
                a_last = a_cum_t[h:h + 1, lc - 1:lc]
                decay = jnp.exp(jnp.where(causal, a_col - a_row, -jnp.inf))
                w_in = (cb * decay * dt_row).astype(BF16)
                state = ssm_ref[h]
                y_h = _dot(w_in, x_h) + jnp.exp(a_col) * _dot_nt(c_g, state.astype(BF16))
                ys.append(y_h)
                w_state = (x_h_t * (dt_row * jnp.exp(a_last - a_row))).astype(BF16)
                ssm_ref[h] = jnp.exp(a_last) * state + _dot(w_state, b_g)
        lanes = slice(g * HEADS_PER_GROUP * SSM_HEAD_DIM, (g + 1) * HEADS_PER_GROUP * SSM_HEAD_DIM)
        y_g = jnp.concatenate(ys, axis=-1) + dskip_ref[:, lanes] * conv[:, lanes]
        y_g = y_g * _silu(z_ref[:, lanes])
        ms = jnp.mean(y_g * y_g, axis=-1, keepdims=True)
        y_ref[:, lanes] = (y_g * lax.rsqrt(ms + EPS) * ng_ref[:, lanes]).astype(y_ref.dtype)


def ssd(proj, conv0, ssm0, conv_w, conv_b, dt_bias, a_log, d_skip, norm_gain, *, batch, seq, row0, lc):
    nc = seq // lc
    tok = lambda width, col: pl.BlockSpec((lc, width), lambda b, c: (row0 // lc + b * nc + c, col // width))
    const = lambda r, w: pl.BlockSpec((r, w), lambda b, c: (0, 0))
    kern = functools.partial(_ssd_kernel, lc=lc)
    return pl.pallas_call(
        kern,
        out_shape=[jax.ShapeDtypeStruct((batch * seq, D_INNER), BF16),
                   jax.ShapeDtypeStruct((batch, SSM_HEADS, SSM_HEAD_DIM, D_STATE), F32),
                   jax.ShapeDtypeStruct((batch, CONV_W - 1, CONV_DIM), F32)],
        grid=(batch, nc),
        in_specs=[tok(CONV_DIM, COL_XBC), tok(D_INNER, COL_Z), tok(LANES, COL_DT),
                  pl.BlockSpec((None, CONV_W - 1, CONV_DIM), lambda b, c: (b, 0, 0)),
                  pl.BlockSpec((None, SSM_HEADS, SSM_HEAD_DIM, D_STATE), lambda b, c: (b, 0, 0, 0)),
                  const(CONV_W, CONV_DIM), const(1, CONV_DIM), const(1, LANES), const(1, LANES),
                  const(1, D_INNER), const(1, D_INNER)],
        out_specs=[pl.BlockSpec((lc, D_INNER), lambda b, c: (b * nc + c, 0)),
                   pl.BlockSpec((None, SSM_HEADS, SSM_HEAD_DIM, D_STATE), lambda b, c: (b, 0, 0, 0)),
                   pl.BlockSpec((None, CONV_W - 1, CONV_DIM), lambda b, c: (b, 0, 0))],
        scratch_shapes=[pltpu.VMEM((SUBLANES + lc, CONV_DIM), F32)],
        compiler_params=_params(("parallel", "arbitrary")),
        name="ssd",
    )(proj, proj, proj, conv0, ssm0, conv_w, conv_b, dt_bias, a_log, d_skip, norm_gain)


def _out_proj_kernel(x_ref, att_ref, y_ref, ga_ref, gm_ref, wa_ref, wm_ref, wo_ref, o_ref):
    merged = (_sigmoid(ga_ref[...]) * _dot(att_ref[...], wa_ref[...])
              + _sigmoid(gm_ref[...]) * _dot(y_ref[...], wm_ref[...]))
    o_ref[...] = x_ref[...] + _dot(merged.astype(BF16), wo_ref[...])


def out_proj(x, att, y, proj, wa, wm, wo, *, tm):
    t = x.shape[0]
    gate = lambda c: pl.BlockSpec((tm, D_MODEL), lambda i, c=c: (i, c // D_MODEL))
    const = lambda r: pl.BlockSpec((r, D_MODEL), lambda i: (0, 0))
    return pl.pallas_call(
        _out_proj_kernel,
        out_shape=jax.ShapeDtypeStruct((t, D_MODEL), F32),
        grid=(t // tm,),
        in_specs=[pl.BlockSpec((tm, D_MODEL), lambda i: (i, 0)),
                  pl.BlockSpec((tm, ATT_DIM), lambda i: (i, 0)),
                  pl.BlockSpec((tm, D_INNER), lambda i: (i, 0)),
                  gate(COL_GATE), gate(COL_GATE + D_MODEL),
                  const(ATT_DIM), const(D_INNER), const(D_MODEL)],
        out_specs=pl.BlockSpec((tm, D_MODEL), lambda i: (i, 0)),
        compiler_params=_params(("parallel",)),
        name="out_proj",
    )(x, att, y, proj, proj, wa, wm, wo)


def _top_values(s, k):
    vals = []
    for _ in range(k):
        m = jnp.max(s, axis=0, keepdims=True)
        vals.append(m)
        s = jnp.where(s == m, -jnp.inf, s)
    return vals


def _top_ranked(s, k):
    vals = []
    rank = jnp.full(s.shape, float(k), F32)
    for i in range(k):
        m = jnp.max(s, axis=0, keepdims=True)
        hit = s == m
        vals.append(m)
        rank = jnp.where(hit, float(i), rank)
        s = jnp.where(hit, -jnp.inf, s)
    return vals, rank


def _peer_keys_kernel(x_ref, g_ref, wqt_ref, k1_ref, k2_ref, ht_ref, e1_ref, cnt_ref, e2_ref, r2_ref):
    x = x_ref[...]
    ms = jnp.mean(x * x, axis=-1, keepdims=True)
    h32 = x * lax.rsqrt(ms + EPS) * g_ref[...]
    h_t = h32.T.astype(BF16)
    ht_ref[...] = h_t
    q_t = _dot(wqt_ref[...], h_t)
    for hd in range(PK_HEADS):
        r0 = hd * 2 * PK_HALF
        s1 = _dot(k1_ref[hd], q_t[r0:r0 + PK_HALF, :].astype(BF16))
        s2 = _dot(k2_ref[hd], q_t[r0 + PK_HALF:r0 + 2 * PK_HALF, :].astype(BF16))
        v1 = _top_values(s1, PK_TOPK)
        v2, r2 = _top_ranked(s2, PK_TOPK)
        widths = [PK_TOPK // (a + 1) for a in range(PK_TOPK)]
        pad_rows = -sum(widths) % SUBLANES
        cand = jnp.concatenate([v1[a] + jnp.concatenate(v2[:widths[a]], axis=0) for a in range(PK_TOPK)]
                               + [jnp.full((pad_rows, s1.shape[1]), -jnp.inf, F32)], axis=0)
        tau = _top_values(cand, PK_TOPK)[-1]
        top = v1[0] + v2[0]
        chosen = cand >= tau
        z = jnp.sum(jnp.where(chosen, jnp.exp(cand - top), 0.0), axis=0, keepdims=True)
        cnt = jnp.zeros(s1.shape, F32)
        row0 = 0
        for a in range(PK_TOPK):
            n_a = jnp.sum(chosen[row0:row0 + widths[a]].astype(F32), axis=0, keepdims=True)
            cnt = jnp.where(s1 == v1[a], n_a, cnt)
            row0 += widths[a]
        e1_ref[hd] = jnp.exp(s1 - v1[0]) / z
        cnt_ref[hd] = cnt
        e2_ref[hd] = jnp.exp(s2 - v2[0])
        r2_ref[hd] = r2


def peer_keys(x, gain, wq_t, k1, k2, *, tt):
    t = x.shape[0]
    per_head = pl.BlockSpec((PK_HEADS, N_KEYS, tt), lambda i: (0, 0, i))
    table = jax.ShapeDtypeStruct((PK_HEADS, N_KEYS, t), F32)
    return pl.pallas_call(
        _peer_keys_kernel,
        out_shape=[jax.ShapeDtypeStruct((D_MODEL, t), BF16), table, table, table, table],
        grid=(t // tt,),
        in_specs=[pl.BlockSpec((tt, D_MODEL), lambda i: (i, 0)),
                  pl.BlockSpec((1, D_MODEL), lambda i: (0, 0)),
                  pl.BlockSpec((PK_HEADS * 2 * PK_HALF, D_MODEL), lambda i: (0, 0)),
                  pl.BlockSpec((PK_HEADS, N_KEYS, PK_HALF), lambda i: (0, 0, 0)),
                  pl.BlockSpec((PK_HEADS, N_KEYS, PK_HALF), lambda i: (0, 0, 0))],
        out_specs=[pl.BlockSpec((D_MODEL, tt), lambda i: (0, i)), per_head, per_head, per_head, per_head],
        compiler_params=_params(("parallel",)),
        name="peer_keys",
    )(x, gain, wq_t, k1, k2)


def _gelu(x):
    return 0.5 * x * (1.0 + lax.erf(x * (2.0 ** -0.5)))


def _peer_mix_kernel(x_ref, h_ref, u_ref, vt_ref, e1_ref, cnt_ref, e2_ref, r2_ref, o_ref,
                     acc_scr, raw_scr, coef_scr, e2_scr, r2_scr, *, eb):
    e = pl.program_id(1)
    tt = acc_scr.shape[1]
    n_sub = eb // PEER_SUB
    rows_per_sub = PEER_SUB // N_KEYS
    pack = 2 * SUBLANES

    @pl.when(e == 0)
    def _():
        acc_scr[...] = jnp.zeros(acc_scr.shape, F32)
        for hd in range(PK_HEADS):
            e2_scr[hd] = e2_ref[hd].astype(BF16)
            r2_scr[hd] = r2_ref[hd].astype(BF16)

    def pre_activations(j):
        start = pl.multiple_of(j * PEER_SUB, PEER_SUB)
        return _dot(u_ref[pl.ds(start, PEER_SUB), :], h_ref[...])

    def mix(slot, j):
        zero = jnp.zeros((pack, LANES), BF16)
        for r in range(rows_per_sub):
            i1 = (e * n_sub + j) * rows_per_sub + r
            cnt_rows = [cnt_ref[hd, pl.ds(i1, 1), :] for hd in range(PK_HEADS)]
            e1_rows = [e1_ref[hd, pl.ds(i1, 1), :] for hd in range(PK_HEADS)]
            for lt in range(tt // LANES):
                lanes = slice(lt * LANES, (lt + 1) * LANES)
                cnt_b = [jnp.broadcast_to(row[:, lanes], (pack, LANES)).astype(BF16) for row in cnt_rows]
                e1_b = [jnp.broadcast_to(row[:, lanes], (pack, LANES)).astype(BF16) for row in e1_rows]
                for c in range(N_KEYS // pack):
                    keys = slice(c * pack, (c + 1) * pack)
                    gate = zero
                    for hd in range(PK_HEADS):
                        keep = r2_scr[hd, keys, lanes] < cnt_b[hd]
                        gate = gate + jnp.where(keep, e2_scr[hd, keys, lanes] * e1_b[hd], zero)
                    rows = slice(r * N_KEYS + c * pack, r * N_KEYS + (c + 1) * pack)
                    coef_scr[slot, rows, lanes] = gate * _gelu(raw_scr[slot, rows, lanes]).astype(BF16)

    def accumulate(slot, j):
        acc_scr[...] += _dot(vt_ref[j], coef_scr[slot])

    raw_scr[0] = pre_activations(0)
    coef_scr[1] = jnp.zeros(coef_scr.shape[1:], BF16)

    def body(pair, carry):
        j = 2 * pair
        raw_scr[1] = pre_activations(j + 1)
        mix(0, j)
        accumulate(1, jnp.maximum(j - 1, 0))
        raw_scr[0] = pre_activations(jnp.minimum(j + 2, n_sub - 1))
        mix(1, j + 1)
        accumulate(0, j)
        return carry

    lax.fori_loop(0, n_sub // 2, body, 0)
    accumulate(1, n_sub - 1)

    @pl.when(e == pl.num_programs(1) - 1)
    def _():
        o_ref[...] = x_ref[...] + acc_scr[...].T


def peer_mix(x, h, u, v_sub, e1, cnt, e2, r2, *, tt, eb):
    t = x.shape[0]
    assert (eb // PEER_SUB) % 2 == 0
    per_head = pl.BlockSpec((PK_HEADS, N_KEYS, tt), lambda i, e: (0, 0, i))
    kern = functools.partial(_peer_mix_kernel, eb=eb)
    return pl.pallas_call(
        kern,
        out_shape=jax.ShapeDtypeStruct((t, D_MODEL), F32),
        grid=(t // tt, N_EXPERTS // eb),
        in_specs=[pl.BlockSpec((tt, D_MODEL), lambda i, e: (i, 0)),
                  pl.BlockSpec((D_MODEL, tt), lambda i, e: (0, i)),
                  pl.BlockSpec((eb, D_MODEL), lambda i, e: (e, 0)),
                  pl.BlockSpec((eb // PEER_SUB, D_MODEL, PEER_SUB), lambda i, e: (e, 0, 0)),
                  per_head, per_head, per_head, per_head],
        out_specs=pl.BlockSpec((tt, D_MODEL), lambda i, e: (i, 0)),
        scratch_shapes=[pltpu.VMEM((D_MODEL, tt), F32), pltpu.VMEM((2, PEER_SUB, tt), F32),
                        pltpu.VMEM((2, PEER_SUB, tt), BF16),
                        pltpu.VMEM((PK_HEADS, N_KEYS, tt), BF16), pltpu.VMEM((PK_HEADS, N_KEYS, tt), BF16)],
        compiler_params=_params(("parallel", "arbitrary")),
        name="peer_mix",
    )(x, h, u, v_sub, e1, cnt, e2, r2)


def _rope_tables(pos):
    half = HEAD_DIM // 2
    inv_freq = ROPE_THETA ** (-jnp.arange(half, dtype=F32) / half)
    ang = pos[:, None] * inv_freq[None, :]
    cos, sin = jnp.cos(ang), jnp.sin(ang)
    reps = LANES // HEAD_DIM
    return jnp.tile(jnp.concatenate([cos, cos], -1), (1, reps)), jnp.tile(jnp.concatenate([-sin, sin], -1), (1, reps))


def _lambda_init(layer):
    return 0.8 - 0.6 * math.exp(-0.3 * layer)


def _pad_lanes(v, width=LANES):
    return jnp.pad(v, (0, width - v.shape[0])).reshape(1, width)


def kernel(x_prompt, x_sample, cache_k, cache_v, state_ssm, state_conv, norm_mix, w_in, q_norm, k_norm, lam_q1, lam_k1, lam_q2, lam_k2, attn_subln, conv_w, conv_b, dt_bias, a_log, d_skip, ssm_norm, w_proj_attn, w_proj_ssm, w_out, norm_ffn, peer_wq, peer_k1, peer_k2, peer_u, peer_v):
    bp, lp, _ = x_prompt.shape
    bs, ls, _ = x_sample.shape
    depth = w_in.shape[0]
    n_past = cache_k.shape[2]
    tp, ts = bp * lp, bs * ls
    x = jnp.concatenate([x_prompt.reshape(tp, D_MODEL), x_sample.reshape(ts, D_MODEL)], axis=0)
    t_all = tp + ts
    tm = math.gcd(t_all, 1024)
    tt = math.gcd(t_all, 256)

    cos_p, sin_p = _rope_tables(jnp.tile(jnp.arange(lp, dtype=F32), bp))
    cos_s, sin_s = _rope_tables(jnp.tile(n_past + jnp.arange(ls, dtype=F32), bs))
    cache_k = cache_k.reshape(depth, bs, n_past, ATT_DIM)
    cache_v = cache_v.reshape(depth, bs, n_past, ATT_DIM)
    conv0_p = jnp.zeros((bp, CONV_W - 1, CONV_DIM), F32)
    ssm0_p = jnp.zeros((bp, SSM_HEADS, SSM_HEAD_DIM, D_STATE), F32)

    outs = {name: [] for name in ("kp", "vp", "sp", "cp", "ks", "vs", "ss", "cs")}
    for i in range(depth):
        li = _lambda_init(i)
        w_i = w_in[i].astype(BF16)
        w_qkv = w_i[:, :3 * ATT_DIM]
        w_rest = jnp.zeros((D_MODEL, PROJ_COLS), BF16)
        for col, piece in ((COL_Z, w_i[:, 3072:5120]), (COL_GATE, w_i[:, 9248:]), (COL_XBC, w_i[:, 5120:9216]),
                           (COL_DT, w_i[:, 9216:9248])):
            w_rest = lax.dynamic_update_slice(w_rest, piece, (0, col))
        gain_mix = norm_mix[i].reshape(1, D_MODEL)
        proj = in_proj(x, gain_mix, w_rest, tm=tm)

        q_gain = jnp.tile(q_norm[i], LANES // HEAD_DIM).reshape(1, LANES)
        k_gain = jnp.tile(k_norm[i], LANES // HEAD_DIM).reshape(1, LANES)
        qkv_w = (gain_mix, w_qkv)
        qb_p, k_p, kb_p, v_p, vb_p = qkv_proj(x, *qkv_w, cos_p, sin_p, q_gain, k_gain, row0=0, rows=tp,
                                              tm=math.gcd(tp, 512))
        qb_s, k_s, kb_s, v_s, vb_s = qkv_proj(x, *qkv_w, cos_s, sin_s, q_gain, k_gain, row0=tp, rows=ts,
                                              tm=math.gcd(math.gcd(tp, ts), 512))

        lam_rows = jnp.stack([lam_q1[i], lam_k1[i], lam_q2[i], lam_k2[i]])
        sub_gain = attn_subln[i].reshape(1, LANES)
        att_p = attn_prompt(qb_p, kb_p, vb_p, lam_rows, sub_gain, li, batch=bp, seq=lp,
                            tq=min(ATT_TQ, lp), tk=min(ATT_TK, lp), hps=ATT_HPS)
        att_s = attn_sample(qb_s, kb_s, vb_s, cache_k, cache_v, lam_rows, sub_gain, li,
                            layer=i, batch=bs, rows=ls, row0=0, tk=min(ATT_TKS, n_past))
        att = jnp.concatenate([att_p, att_s], axis=0)

        ssd_w = (conv_w[i], conv_b[i].reshape(1, CONV_DIM), _pad_lanes(dt_bias[i]), _pad_lanes(a_log[i]),
                 jnp.repeat(d_skip[i], SSM_HEAD_DIM).reshape(1, D_INNER), ssm_norm[i].reshape(1, D_INNER))
        y_p, ssm_p, conv_p = ssd(proj, conv0_p, ssm0_p, *ssd_w, batch=bp, seq=lp, row0=0, lc=min(CHUNK, lp))
        y_s, ssm_s, conv_s = ssd(proj, state_conv[i], state_ssm[i], *ssd_w, batch=bs, seq=ls, row0=tp,
                                 lc=min(CHUNK, ls))
        y = jnp.concatenate([y_p, y_s], axis=0)

        x = out_proj(x, att, y, proj, w_proj_attn[i].astype(BF16), w_proj_ssm[i].astype(BF16),
                     w_out[i].astype(BF16), tm=min(tm, 512))

        h2, *tables = peer_keys(x, norm_ffn[i].reshape(1, D_MODEL), peer_wq[i].T.astype(BF16),
                                peer_k1[i].astype(BF16), peer_k2[i].astype(BF16), tt=tt)
        v_sub = peer_v[i].astype(BF16).reshape(N_EXPERTS // PEER_SUB, PEER_SUB, D_MODEL).transpose(0, 2, 1)
        x = peer_mix(x, h2, peer_u[i].astype(BF16), v_sub, *tables, tt=tt, eb=PEER_EB)

        outs["kp"].append(k_p.reshape(bp, lp, N_HEADS, 2, HEAD_DIM))
        outs["vp"].append(v_p.reshape(bp, lp, N_HEADS, 2 * HEAD_DIM))
        outs["sp"].append(ssm_p)
        outs["cp"].append(conv_p)
        outs["ks"].append(k_s.reshape(bs, ls, N_HEADS, 2, HEAD_DIM))
        outs["vs"].append(v_s.reshape(bs, ls, N_HEADS, 2 * HEAD_DIM))
        outs["ss"].append(ssm_s)
        outs["cs"].append(conv_s)

    stack = lambda name: jnp.stack(outs[name])
    return (x[:tp].reshape(bp, lp, D_MODEL), x[tp:].reshape(bs, ls, D_MODEL),
            stack("kp"), stack("vp"), stack("sp"), stack("cp"),
            stack("ks"), stack("vs"), stack("ss"), stack("cs"))
```

```python
import functools
import math

import jax
import jax.numpy as jnp
from jax import lax
from jax.experimental import pallas as pl
from jax.experimental.pallas import tpu as pltpu

F32 = jnp.float32
BF16 = jnp.bfloat16

D_MODEL = 1024
CHUNK = 64
N_HEADS = 8
HEAD_DIM = 64
ATT_DIM = N_HEADS * 2 * HEAD_DIM
ROPE_THETA = 10000.0
D_INNER = 2048
SSM_HEAD_DIM = 64
SSM_HEADS = D_INNER // SSM_HEAD_DIM
SSM_GROUPS = 8
HEADS_PER_GROUP = SSM_HEADS // SSM_GROUPS
D_STATE = 128
CONV_W = 4
CONV_DIM = D_INNER + 2 * SSM_GROUPS * D_STATE
PK_HEADS = 8
N_KEYS = 128
N_EXPERTS = N_KEYS * N_KEYS
PK_TOPK = 16
PK_HALF = 128
EPS = 1e-6

LANES = 128
SUBLANES = 8
V7X_VMEM_BYTES = 64 * 1024 * 1024
VMEM_LIMIT = V7X_VMEM_BYTES * 7 // 8

COL_Z = 0
COL_GATE = 2048
COL_XBC = 4096
COL_DT = 8192
PROJ_COLS = 8320
PROJ_TN = 1664
ATT_TQ = 512
ATT_TK = 512
ATT_TKS = 1024
ATT_HPS = 2
PEER_SUB = 256
PEER_EB = 4096

HIGHEST = lax.Precision.HIGHEST
NT_DIMS = (((1,), (1,)), ((), ()))


def _params(sem):
    return pltpu.CompilerParams(dimension_semantics=sem, vmem_limit_bytes=VMEM_LIMIT)


def _dot(a, b):
    return jnp.dot(a, b, preferred_element_type=F32)


def _dot_nt(a, b):
    return lax.dot_general(a, b, NT_DIMS, preferred_element_type=F32)


def _dot_exact(a, b):
    return jnp.dot(a, b, preferred_element_type=F32, precision=HIGHEST)


def _dot_split(a, b):
    head = a.astype(BF16)
    rest = (a - head.astype(F32)).astype(BF16)
    return _dot(head, b) + _dot(rest, b)


def _transpose_exact(x):
    c = x.shape[1]
    eye = (lax.broadcasted_iota(jnp.int32, (c, c), 0) == lax.broadcasted_iota(jnp.int32, (c, c), 1)).astype(F32)
    return lax.dot_general(eye, x, NT_DIMS, preferred_element_type=F32, precision=HIGHEST)


def _sigmoid(x):
    return 1.0 / (1.0 + jnp.exp(-x))


def _silu(x):
    return x * _sigmoid(x)


def _softplus(x):
    return jnp.maximum(x, 0.0) + jnp.log(1.0 + jnp.exp(-jnp.abs(x)))


def _in_proj_kernel(x_ref, g_ref, w_ref, o_ref, h_scr):
    @pl.when(pl.program_id(1) == 0)
    def _():
        x = x_ref[...]
        ms = jnp.mean(x * x, axis=-1, keepdims=True)
        h_scr[...] = (x * lax.rsqrt(ms + EPS) * g_ref[...]).astype(BF16)

    o_ref[...] = _dot(h_scr[...], w_ref[...])


def in_proj(x, gain, w, *, tm):
    t = x.shape[0]
    n = w.shape[1]
    return pl.pallas_call(
        _in_proj_kernel,
        out_shape=jax.ShapeDtypeStruct((t, n), F32),
        grid=(t // tm, n // PROJ_TN),
        in_specs=[pl.BlockSpec((tm, D_MODEL), lambda i, j: (i, 0)),
                  pl.BlockSpec((1, D_MODEL), lambda i, j: (0, 0)),
                  pl.BlockSpec((D_MODEL, PROJ_TN), lambda i, j: (0, j))],
        out_specs=pl.BlockSpec((tm, PROJ_TN), lambda i, j: (i, j)),
        scratch_shapes=[pltpu.VMEM((tm, D_MODEL), BF16)],
        compiler_params=_params(("parallel", "arbitrary")),
        name="in_proj",
    )(x, gain, w)


def _qkv_proj_kernel(x_ref, g_ref, w_ref, cos_ref, sin_ref, qg_ref, kg_ref,
                     qb_ref, ko_ref, kb_ref, vo_ref, vb_ref):
    x = x_ref[...]
    ms = jnp.mean(x * x, axis=-1, keepdims=True)
    h = (x * lax.rsqrt(ms + EPS) * g_ref[...]).astype(BF16)
    qkv = _dot(h, w_ref[...])
    cos = cos_ref[...]
    sin = sin_ref[...]
    row = lax.broadcasted_iota(jnp.int32, (LANES, LANES), 0) // HEAD_DIM
    col = lax.broadcasted_iota(jnp.int32, (LANES, LANES), 1) // HEAD_DIM
    group_mean = jnp.where(row == col, 1.0 / HEAD_DIM, 0.0).astype(BF16)
    lane = lax.broadcasted_iota(jnp.int32, (1, LANES), 1)
    low_half = (lane % HEAD_DIM) < (HEAD_DIM // 2)

    def norm_rope(y, gain):
        ms = _dot_split(y * y, group_mean)
        y = y * lax.rsqrt(ms + EPS) * gain
        partner = jnp.where(low_half, pltpu.roll(y, LANES - HEAD_DIM // 2, 1), pltpu.roll(y, HEAD_DIM // 2, 1))
        return y * cos + partner * sin

    for h_i in range(N_HEADS):
        sl = slice(h_i * LANES, (h_i + 1) * LANES)
        q = norm_rope(qkv[:, h_i * LANES:(h_i + 1) * LANES], qg_ref[...]) * (HEAD_DIM ** -0.5)
        qb_ref[:, sl] = q.astype(BF16)
        k = norm_rope(qkv[:, ATT_DIM + h_i * LANES:ATT_DIM + (h_i + 1) * LANES], kg_ref[...])
        ko_ref[:, sl] = k
        kb_ref[:, sl] = k.astype(BF16)
    v = qkv[:, 2 * ATT_DIM:]
    vo_ref[...] = v
    vb_ref[...] = v.astype(BF16)


def qkv_proj(x, gain, w_qkv, cos, sin, q_gain, k_gain, *, row0, rows, tm):
    t = rows
    tile0 = row0 // tm
    tab = pl.BlockSpec((tm, LANES), lambda i: (i, 0))
    vec = pl.BlockSpec((1, LANES), lambda i: (0, 0))
    out = pl.BlockSpec((tm, ATT_DIM), lambda i: (i, 0))
    return pl.pallas_call(
        _qkv_proj_kernel,
        out_shape=[jax.ShapeDtypeStruct((t, ATT_DIM), BF16),
                   jax.ShapeDtypeStruct((t, ATT_DIM), F32),
                   jax.ShapeDtypeStruct((t, ATT_DIM), BF16),
                   jax.ShapeDtypeStruct((t, ATT_DIM), F32),
                   jax.ShapeDtypeStruct((t, ATT_DIM), BF16)],
        grid=(t // tm,),
        in_specs=[pl.BlockSpec((tm, D_MODEL), lambda i: (tile0 + i, 0)),
                  pl.BlockSpec((1, D_MODEL), lambda i: (0, 0)),
                  pl.BlockSpec((D_MODEL, 3 * ATT_DIM), lambda i: (0, 0)),
                  tab, tab, vec, vec],
        out_specs=[out, out, out, out, out],
        compiler_params=_params(("parallel",)),
        name="qkv_proj",
    )(x, gain, w_qkv, cos, sin, q_gain, k_gain)


def _lambda_full(lam_ref, lam_init):
    a = jnp.sum(lam_ref[0:1, :] * lam_ref[1:2, :], axis=-1, keepdims=True)
    b = jnp.sum(lam_ref[2:3, :] * lam_ref[3:4, :], axis=-1, keepdims=True)
    return jnp.exp(a) - jnp.exp(b) + lam_init


def _split_components(q):
    lane = lax.broadcasted_iota(jnp.int32, (1, LANES), 1)
    zero = jnp.zeros_like(q)
    return jnp.concatenate([jnp.where(lane < HEAD_DIM, q, zero), jnp.where(lane >= HEAD_DIM, q, zero)], axis=0)


def _diff_combine(acc, l, rows, lam, gain, lam_init):
    o1 = acc[:rows] / l[:rows]
    o2 = acc[rows:] / l[rows:]
    o = o1 - lam * o2
    ms = jnp.mean(o * o, axis=-1, keepdims=True)
    return o * lax.rsqrt(ms + EPS) * gain * (1.0 - lam_init)


def _attn_prompt_kernel(lam_ref, g_ref, q_ref, k_ref, v_ref, o_ref,
                        vext_scr, s_scr, p_scr, m_scr, alpha_scr, acc_scr, *, tq, tk, hps, lam_init):
    i = pl.program_id(2)
    rows = 2 * tq
    heads = range(hps)

    @pl.when(i == 0)
    def _():
        for hh in heads:
            vext_scr[hh, :, :LANES] = v_ref[:, hh * LANES:(hh + 1) * LANES]
            vext_scr[hh, :, LANES:] = jnp.ones((vext_scr.shape[1], LANES), BF16)

    qq = [_split_components(q_ref[:, hh * LANES:(hh + 1) * LANES]) for hh in heads]
    diag = (i * tq) // tk
    n_blocks = diag + 1

    def key_block(t):
        return jnp.where(t == 0, diag, t - 1)

    def scores(hh, blk):
        return _dot_nt(qq[hh], k_ref[pl.ds(pl.multiple_of(blk * tk, tk), tk), hh * LANES:(hh + 1) * LANES])

    def softmax(hh, slot, bias=None):
        s = s_scr[hh, slot]
        if bias is not None:
            s = s + bias
        m_old = m_scr[hh]
        m_new = jnp.maximum(m_old, jnp.max(s, axis=-1, keepdims=True))
        alpha_scr[hh, slot] = jnp.exp(m_old - m_new)
        for c in range(tk // LANES):
            sl = slice(c * LANES, (c + 1) * LANES)
            p_scr[hh, slot, :, sl] = jnp.exp(s[:, sl] - m_new).astype(BF16)
        m_scr[hh] = m_new

    def accumulate(hh, slot, blk):
        pv = _dot(p_scr[hh, slot], vext_scr[hh, pl.ds(pl.multiple_of(blk * tk, tk), tk), :])
        alpha = alpha_scr[hh, slot]
        acc_scr[hh, :, :LANES] = alpha * acc_scr[hh, :, :LANES] + pv[:, :LANES]
        acc_scr[hh, :, LANES:] = alpha * acc_scr[hh, :, LANES:] + pv[:, LANES:]

    m_scr[...] = jnp.full(m_scr.shape, -jnp.inf, F32)
    acc_scr[...] = jnp.zeros(acc_scr.shape, F32)
    q_chunk = (i * tq + lax.broadcasted_iota(jnp.int32, (rows, 1), 0) % tq) // CHUNK
    k_chunk = (diag * tk + lax.broadcasted_iota(jnp.int32, (1, tk), 1)) // CHUNK
    for hh in heads:
        p_scr[hh, 1] = jnp.zeros(p_scr.shape[2:], BF16)
        alpha_scr[hh, 1] = jnp.ones(alpha_scr.shape[2:], F32)
        s_scr[hh, 0] = jnp.where(k_chunk <= q_chunk, scores(hh, diag), -jnp.inf)

    last = n_blocks - 1

    def body(pair, carry):
        t = 2 * pair
        for hh in heads:
            s_scr[hh, 1] = scores(hh, key_block(jnp.minimum(t + 1, last)))
            softmax(hh, 0)
            accumulate(hh, 1, key_block(jnp.maximum(t - 1, 0)))
        for hh in heads:
            s_scr[hh, 0] = scores(hh, key_block(jnp.minimum(t + 2, last)))
            softmax(hh, 1, jnp.where(t + 1 <= last, 0.0, -jnp.inf))
            accumulate(hh, 0, key_block(t))
        return carry

    n_pairs = (n_blocks + 1) // 2
    lax.fori_loop(0, n_pairs, body, 0)
    lam = _lambda_full(lam_ref, lam_init)
    for hh in heads:
        accumulate(hh, 1, key_block(jnp.minimum(2 * n_pairs - 1, last)))
        acc = acc_scr[hh]
        o_ref[:, hh * LANES:(hh + 1) * LANES] = _diff_combine(
            acc[:, :LANES], acc[:, LANES:], tq, lam, g_ref[...], lam_init).astype(o_ref.dtype)


def attn_prompt(qb, kb, vb, lam_rows, sub_gain, lam_init, *, batch, seq, tq, tk, hps):
    nq = seq // tq
    assert tk % tq == 0 and tq % CHUNK == 0, "one key block must hold all chunks of a query block"
    assert N_HEADS % hps == 0
    kern = functools.partial(_attn_prompt_kernel, tq=tq, tk=tk, hps=hps, lam_init=lam_init)
    width = hps * LANES
    return pl.pallas_call(
        kern,
        out_shape=jax.ShapeDtypeStruct((batch * seq, ATT_DIM), BF16),
        grid=(batch, N_HEADS // hps, nq),
        in_specs=[pl.BlockSpec((4, HEAD_DIM), lambda b, h, i: (0, 0)),
                  pl.BlockSpec((1, LANES), lambda b, h, i: (0, 0)),
                  pl.BlockSpec((tq, width), lambda b, h, i: (b * nq + i, h)),
                  pl.BlockSpec((seq, width), lambda b, h, i: (b, h)),
                  pl.BlockSpec((seq, width), lambda b, h, i: (b, h))],
        out_specs=pl.BlockSpec((tq, width), lambda b, h, i: (b * nq + i, h)),
        scratch_shapes=[pltpu.VMEM((hps, seq, 2 * LANES), BF16),
                        pltpu.VMEM((hps, 2, 2 * tq, tk), F32), pltpu.VMEM((hps, 2, 2 * tq, tk), BF16),
                        pltpu.VMEM((hps, 2 * tq, LANES), F32), pltpu.VMEM((hps, 2, 2 * tq, LANES), F32),
                        pltpu.VMEM((hps, 2 * tq, 2 * LANES), F32)],
        compiler_params=_params(("parallel", "parallel", "arbitrary")),
        name="attn_prompt",
    )(lam_rows, sub_gain, qb, kb, vb)


def _attn_sample_kernel(lam_ref, g_ref, q_ref, kn_ref, vn_ref, kt_ref, vc_ref, o_ref,
                        qq_scr, m_scr, l_scr, acc_scr, *, rows, tk, lam_init):
    j = pl.program_id(1)
    hr = 2 * rows

    @pl.when(j == 0)
    def _():
        for h in range(N_HEADS):
            qq_scr[h * hr:(h + 1) * hr, :] = _split_components(q_ref[:, h * LANES:(h + 1) * LANES])
        m_scr[...] = jnp.full(m_scr.shape, -jnp.inf, F32)
        l_scr[...] = jnp.zeros(l_scr.shape, F32)
        acc_scr[...] = jnp.zeros(acc_scr.shape, F32)

    def update(scores_of, values_of):
        s = jnp.concatenate([scores_of(h, qq_scr[h * hr:(h + 1) * hr, :]) for h in range(N_HEADS)], axis=0)
        m_old = m_scr[...]
        m_new = jnp.maximum(m_old, jnp.max(s, axis=-1, keepdims=True))
        alpha = jnp.exp(m_old - m_new)
        p = jnp.exp(s - m_new)
        l_scr[...] = alpha * l_scr[...] + jnp.sum(p, axis=-1, keepdims=True)
        m_scr[...] = m_new
        p = p.astype(BF16)
        for h in range(N_HEADS):
            hs = slice(h * hr, (h + 1) * hr)
            acc_scr[hs, :] = alpha[hs] * acc_scr[hs, :] + _dot(p[hs], values_of(h))

    update(lambda h, qq: _dot(qq, kt_ref[h * LANES:(h + 1) * LANES, :].astype(BF16)),
           lambda h: vc_ref[pl.ds(h, tk, stride=N_HEADS), :].astype(BF16))

    @pl.when(j == pl.num_programs(1) - 1)
    def _():
        update(lambda h, qq: _dot_nt(qq, kn_ref[:, h * LANES:(h + 1) * LANES]),
               lambda h: vn_ref[:, h * LANES:(h + 1) * LANES])
        lam = _lambda_full(lam_ref, lam_init)
        for h in range(N_HEADS):
            hs = slice(h * hr, (h + 1) * hr)
            o_ref[:, h * LANES:(h + 1) * LANES] = _diff_combine(
                acc_scr[hs, :], l_scr[hs, :], rows, lam, g_ref[...], lam_init).astype(o_ref.dtype)


def attn_sample(qb, kb, vb, cache_kt, cache_v, lam_rows, sub_gain, lam_init, *, layer, batch, rows, tk):
    past = cache_kt.shape[3]
    new_blk = pl.BlockSpec((rows, ATT_DIM), lambda b, j: (b, 0))
    kern = functools.partial(_attn_sample_kernel, rows=rows, tk=tk, lam_init=lam_init)
    return pl.pallas_call(
        kern,
        out_shape=jax.ShapeDtypeStruct((batch * rows, ATT_DIM), BF16),
        grid=(batch, past // tk),
        in_specs=[pl.BlockSpec((4, HEAD_DIM), lambda b, j: (0, 0)),
                  pl.BlockSpec((1, LANES), lambda b, j: (0, 0)),
                  new_blk, new_blk, new_blk,
                  pl.BlockSpec((None, None, ATT_DIM, tk), lambda b, j: (layer, b, 0, j)),
                  pl.BlockSpec((None, None, tk * N_HEADS, 2 * HEAD_DIM), lambda b, j: (layer, b, j, 0))],
        out_specs=pl.BlockSpec((rows, ATT_DIM), lambda b, j: (b, 0)),
        scratch_shapes=[pltpu.VMEM((N_HEADS * 2 * rows, LANES), BF16),
                        pltpu.VMEM((N_HEADS * 2 * rows, 1), F32), pltpu.VMEM((N_HEADS * 2 * rows, 1), F32),
                        pltpu.VMEM((N_HEADS * 2 * rows, LANES), F32)],
        compiler_params=_params(("parallel", "arbitrary")),
        name="attn_sample",
    )(lam_rows, sub_gain, qb, kb, vb, cache_kt, cache_v)


def _ssd_kernel(xbc_ref, z_ref, dt_ref, conv0_ref, ssm0_ref, cw_ref, cb_ref, dtb_ref, alog_ref, dskip_ref, ng_ref,
                y_ref, ssm_ref, conv_ref, xpad_scr, *, lc):
    c = pl.program_id(1)
    pad0 = SUBLANES - (CONV_W - 1)

    @pl.when(c == 0)
    def _():
        xpad_scr[pad0:SUBLANES, :] = conv0_ref[...]
        ssm_ref[...] = ssm0_ref[...]

    xpad_scr[SUBLANES:, :] = xbc_ref[...]
    conv = cb_ref[...] + sum(xpad_scr[pad0 + j:pad0 + j + lc, :] * cw_ref[j:j + 1, :] for j in range(CONV_W))
    conv = _silu(conv)
    tail = xpad_scr[lc + pad0:lc + SUBLANES, :]
    xpad_scr[pad0:SUBLANES, :] = tail
    conv_ref[...] = tail

    dt = _softplus(dt_ref[...] + dtb_ref[...])
    da = dt * (-jnp.exp(alog_ref[...]))
    row = lax.broadcasted_iota(jnp.int32, (lc, lc), 0)
    col = lax.broadcasted_iota(jnp.int32, (lc, lc), 1)
    causal = row >= col
    a_cum = _dot_exact(causal.astype(F32), da)
    dt_t = _transpose_exact(dt)
    a_cum_t = _transpose_exact(a_cum)
    eye = (lax.broadcasted_iota(jnp.int32, (LANES, LANES), 0)
           == lax.broadcasted_iota(jnp.int32, (LANES, LANES), 1)).astype(BF16)

    for g in range(SSM_GROUPS):
        b_g = conv[:, D_INNER + g * D_STATE:D_INNER + (g + 1) * D_STATE].astype(BF16)
        c_off = D_INNER + SSM_GROUPS * D_STATE
        c_g = conv[:, c_off + g * D_STATE:c_off + (g + 1) * D_STATE].astype(BF16)
        cb = _dot_nt(c_g, b_g)
        ys = []
        for pair in range(HEADS_PER_GROUP // 2):
            lane0 = (g * HEADS_PER_GROUP + 2 * pair) * SSM_HEAD_DIM
            x_pair = conv[:, lane0:lane0 + LANES].astype(BF16)
            x_pair_t = _dot_nt(eye, x_pair)
            for sub in range(2):
                h = g * HEADS_PER_GROUP + 2 * pair + sub
                x_h = x_pair[:, sub * SSM_HEAD_DIM:(sub + 1) * SSM_HEAD_DIM]
                x_h_t = x_pair_t[sub * SSM_HEAD_DIM:(sub + 1) * SSM_HEAD_DIM, :]
                a_col = a_cum[:, h:h + 1]
                a_row = a_cum_t[h:h + 1, :]
                dt_row = dt_t[h:h + 1, :]
                a_last = a_cum_t[h:h + 1, lc - 1:lc]
                decay = jnp.exp(jnp.where(causal, a_col - a_row, -jnp.inf))
                w_in = (cb * decay * dt_row).astype(BF16)
                state = ssm_ref[h]
                y_h = _dot(w_in, x_h) + jnp.exp(a_col) * _dot_nt(c_g, state.astype(BF16))
                ys.append(y_h)
                w_state = (x_h_t * (dt_row * jnp.exp(a_last - a_row))).astype(BF16)
                ssm_ref[h] = jnp.exp(a_last) * state + _dot(w_state, b_g)
        lanes = slice(g * HEADS_PER_GROUP * SSM_HEAD_DIM, (g + 1) * HEADS_PER_GROUP * SSM_HEAD_DIM)
        y_g = jnp.concatenate(ys, axis=-1) + dskip_ref[:, lanes] * conv[:, lanes]
        y_g = y_g * _silu(z_ref[:, lanes])
        ms = jnp.mean(y_g * y_g, axis=-1, keepdims=True)
        y_ref[:, lanes] = (y_g * lax.rsqrt(ms + EPS) * ng_ref[:, lanes]).astype(y_ref.dtype)


def ssd(proj, conv0, ssm0, conv_w, conv_b, dt_bias, a_log, d_skip, norm_gain, *, batch, seq, row0, lc):
    nc = seq // lc
    tok = lambda width, col: pl.BlockSpec((lc, width), lambda b, c: (row0 // lc + b * nc + c, col // width))
    const = lambda r, w: pl.BlockSpec((r, w), lambda b, c: (0, 0))
    kern = functools.partial(_ssd_kernel, lc=lc)
    return pl.pallas_call(
        kern,
        out_shape=[jax.ShapeDtypeStruct((batch * seq, D_INNER), BF16),
                   jax.ShapeDtypeStruct((batch, SSM_HEADS, SSM_HEAD_DIM, D_STATE), F32),
                   jax.ShapeDtypeStruct((batch, CONV_W - 1, CONV_DIM), F32)],
        grid=(batch, nc),
        in_specs=[tok(CONV_DIM, COL_XBC), tok(D_INNER, COL_Z), tok(LANES, COL_DT),
                  pl.BlockSpec((None, CONV_W - 1, CONV_DIM), lambda b, c: (b, 0, 0)),
                  pl.BlockSpec((None, SSM_HEADS, SSM_HEAD_DIM, D_STATE), lambda b, c: (b, 0, 0, 0)),
                  const(CONV_W, CONV_DIM), const(1, CONV_DIM), const(1, LANES), const(1, LANES),
                  const(1, D_INNER), const(1, D_INNER)],
        out_specs=[pl.BlockSpec((lc, D_INNER), lambda b, c: (b * nc + c, 0)),
                   pl.BlockSpec((None, SSM_HEADS, SSM_HEAD_DIM, D_STATE), lambda b, c: (b, 0, 0, 0)),
                   pl.BlockSpec((None, CONV_W - 1, CONV_DIM), lambda b, c: (b, 0, 0))],
        scratch_shapes=[pltpu.VMEM((SUBLANES + lc, CONV_DIM), F32)],
        compiler_params=_params(("parallel", "arbitrary")),
        name="ssd",
    )(proj, proj, proj, conv0, ssm0, conv_w, conv_b, dt_bias, a_log, d_skip, norm_gain)


def _out_proj_kernel(x_ref, attp_ref, atts_ref, yp_ref, ys_ref, ga_ref, gm_ref, wa_ref, wm_ref, wo_ref, o_ref, *,
                     prompt_tiles):
    is_prompt = pl.program_id(0) < prompt_tiles
    att = jnp.where(is_prompt, attp_ref[...], atts_ref[...])
    y = jnp.where(is_prompt, yp_ref[...], ys_ref[...])
    merged = (_sigmoid(ga_ref[...]) * _dot(att, wa_ref[...]) + _sigmoid(gm_ref[...]) * _dot(y, wm_ref[...]))
    o_ref[...] = x_ref[...] + _dot(merged.astype(BF16), wo_ref[...])


def out_proj(x, att_p, att_s, y_p, y_s, proj, wa, wm, wo, *, tm):
    t = x.shape[0]
    n_p, n_s = att_p.shape[0] // tm, att_s.shape[0] // tm
    assert n_p * tm == att_p.shape[0] and n_s * tm == att_s.shape[0] and n_p + n_s == t // tm
    gate = lambda c: pl.BlockSpec((tm, D_MODEL), lambda i, c=c: (i, c // D_MODEL))
    const = lambda r: pl.BlockSpec((r, D_MODEL), lambda i: (0, 0))
    prompt = lambda w: pl.BlockSpec((tm, w), lambda i: (jnp.minimum(i, n_p - 1), 0))
    sample = lambda w: pl.BlockSpec((tm, w), lambda i: (jnp.maximum(i - n_p, 0), 0))
    return pl.pallas_call(
        functools.partial(_out_proj_kernel, prompt_tiles=n_p),
        out_shape=jax.ShapeDtypeStruct((t, D_MODEL), F32),
        grid=(t // tm,),
        in_specs=[pl.BlockSpec((tm, D_MODEL), lambda i: (i, 0)),
                  prompt(ATT_DIM), sample(ATT_DIM), prompt(D_INNER), sample(D_INNER),
                  gate(COL_GATE), gate(COL_GATE + D_MODEL),
                  const(ATT_DIM), const(D_INNER), const(D_MODEL)],
        out_specs=pl.BlockSpec((tm, D_MODEL), lambda i: (i, 0)),
        compiler_params=_params(("parallel",)),
        name="out_proj",
    )(x, att_p, att_s, y_p, y_s, proj, proj, wa, wm, wo)


def _top_values(s, k):
    vals = []
    for _ in range(k):
        m = jnp.max(s, axis=0, keepdims=True)
        vals.append(m)
        s = jnp.where(s == m, -jnp.inf, s)
    return vals


def _top_ranked(s, k):
    vals = []
    rank = jnp.full(s.shape, float(k), F32)
    for i in range(k):
        m = jnp.max(s, axis=0, keepdims=True)
        hit = s == m
        vals.append(m)
        rank = jnp.where(hit, float(i), rank)
        s = jnp.where(hit, -jnp.inf, s)
    return vals, rank


def _peer_keys_kernel(x_ref, g_ref, wqt_ref, k1_ref, k2_ref, ht_ref, e1_ref, cnt_ref, e2_ref, r2_ref):
    x = x_ref[...]
    ms = jnp.mean(x * x, axis=-1, keepdims=True)
    h32 = x * lax.rsqrt(ms + EPS) * g_ref[...]
    h_t = h32.T.astype(BF16)
    ht_ref[...] = h_t
    q_t = _dot(wqt_ref[...], h_t)
    for hd in range(PK_HEADS):
        r0 = hd * 2 * PK_HALF
        s1 = _dot(k1_ref[hd], q_t[r0:r0 + PK_HALF, :].astype(BF16))
        s2 = _dot(k2_ref[hd], q_t[r0 + PK_HALF:r0 + 2 * PK_HALF, :].astype(BF16))
        v1 = _top_values(s1, PK_TOPK)
        v2, r2 = _top_ranked(s2, PK_TOPK)
        widths = [PK_TOPK // (a + 1) for a in range(PK_TOPK)]
        pad_rows = -sum(widths) % SUBLANES
        cand = jnp.concatenate([v1[a] + jnp.concatenate(v2[:widths[a]], axis=0) for a in range(PK_TOPK)]
                               + [jnp.full((pad_rows, s1.shape[1]), -jnp.inf, F32)], axis=0)
        tau = _top_values(cand, PK_TOPK)[-1]
        top = v1[0] + v2[0]
        chosen = cand >= tau
        z = jnp.sum(jnp.where(chosen, jnp.exp(cand - top), 0.0), axis=0, keepdims=True)
        cnt = jnp.zeros(s1.shape, F32)
        row0 = 0
        for a in range(PK_TOPK):
            n_a = jnp.sum(chosen[row0:row0 + widths[a]].astype(F32), axis=0, keepdims=True)
            cnt = jnp.where(s1 == v1[a], n_a, cnt)
            row0 += widths[a]
        e1_ref[hd] = jnp.exp(s1 - v1[0]) / z
        cnt_ref[hd] = cnt
        e2_ref[hd] = jnp.exp(s2 - v2[0])
        r2_ref[hd] = r2


def peer_keys(x, gain, wq_t, k1, k2, *, tt):
    t = x.shape[0]
    per_head = pl.BlockSpec((PK_HEADS, N_KEYS, tt), lambda i: (0, 0, i))
    table = jax.ShapeDtypeStruct((PK_HEADS, N_KEYS, t), F32)
    return pl.pallas_call(
        _peer_keys_kernel,
        out_shape=[jax.ShapeDtypeStruct((D_MODEL, t), BF16), table, table, table, table],
        grid=(t // tt,),
        in_specs=[pl.BlockSpec((tt, D_MODEL), lambda i: (i, 0)),
                  pl.BlockSpec((1, D_MODEL), lambda i: (0, 0)),
                  pl.BlockSpec((PK_HEADS * 2 * PK_HALF, D_MODEL), lambda i: (0, 0)),
                  pl.BlockSpec((PK_HEADS, N_KEYS, PK_HALF), lambda i: (0, 0, 0)),
                  pl.BlockSpec((PK_HEADS, N_KEYS, PK_HALF), lambda i: (0, 0, 0))],
        out_specs=[pl.BlockSpec((D_MODEL, tt), lambda i: (0, i)), per_head, per_head, per_head, per_head],
        compiler_params=_params(("parallel",)),
        name="peer_keys",
    )(x, gain, wq_t, k1, k2)


def _gelu(x):
    return 0.5 * x * (1.0 + lax.erf(x * (2.0 ** -0.5)))


def _peer_mix_kernel(x_ref, h_ref, u_ref, vt_ref, e1_ref, cnt_ref, e2_ref, r2_ref, o_ref,
                     acc_scr, raw_scr, coef_scr, e2_scr, r2_scr, *, eb):
    e = pl.program_id(1)
    tt = acc_scr.shape[1]
    n_sub = eb // PEER_SUB
    rows_per_sub = PEER_SUB // N_KEYS
    pack = 2 * SUBLANES

    @pl.when(e == 0)
    def _():
        acc_scr[...] = jnp.zeros(acc_scr.shape, F32)
        for hd in range(PK_HEADS):
            e2_scr[hd] = e2_ref[hd].astype(BF16)
            r2_scr[hd] = r2_ref[hd].astype(BF16)

    def pre_activations(j):
        start = pl.multiple_of(j * PEER_SUB, PEER_SUB)
        return _dot(u_ref[pl.ds(start, PEER_SUB), :], h_ref[...])

    def mix(slot, j):
        zero = jnp.zeros((pack, LANES), BF16)
        for r in range(rows_per_sub):
            i1 = (e * n_sub + j) * rows_per_sub + r
            cnt_rows = [cnt_ref[hd, pl.ds(i1, 1), :] for hd in range(PK_HEADS)]
            e1_rows = [e1_ref[hd, pl.ds(i1, 1), :] for hd in range(PK_HEADS)]
            for lt in range(tt // LANES):
                lanes = slice(lt * LANES, (lt + 1) * LANES)
                cnt_b = [jnp.broadcast_to(row[:, lanes], (pack, LANES)).astype(BF16) for row in cnt_rows]
                e1_b = [jnp.broadcast_to(row[:, lanes], (pack, LANES)).astype(BF16) for row in e1_rows]
                for c in range(N_KEYS // pack):
                    keys = slice(c * pack, (c + 1) * pack)
                    gate = zero
                    for hd in range(PK_HEADS):
                        keep = r2_scr[hd, keys, lanes] < cnt_b[hd]
                        gate = gate + jnp.where(keep, e2_scr[hd, keys, lanes] * e1_b[hd], zero)
                    rows = slice(r * N_KEYS + c * pack, r * N_KEYS + (c + 1) * pack)
                    coef_scr[slot, rows, lanes] = gate * _gelu(raw_scr[slot, rows, lanes]).astype(BF16)

    def accumulate(slot, j):
        acc_scr[...] += _dot(vt_ref[j], coef_scr[slot])

    raw_scr[0] = pre_activations(0)
    coef_scr[1] = jnp.zeros(coef_scr.shape[1:], BF16)

    def body(pair, carry):
        j = 2 * pair
        raw_scr[1] = pre_activations(j + 1)
        mix(0, j)
        accumulate(1, jnp.maximum(j - 1, 0))
        raw_scr[0] = pre_activations(jnp.minimum(j + 2, n_sub - 1))
        mix(1, j + 1)
        accumulate(0, j)
        return carry

    lax.fori_loop(0, n_sub // 2, body, 0)
    accumulate(1, n_sub - 1)

    @pl.when(e == pl.num_programs(1) - 1)
    def _():
        o_ref[...] = x_ref[...] + acc_scr[...].T


def peer_mix(x, h, u, v_sub, e1, cnt, e2, r2, *, tt, eb):
    t = x.shape[0]
    assert (eb // PEER_SUB) % 2 == 0
    per_head = pl.BlockSpec((PK_HEADS, N_KEYS, tt), lambda i, e: (0, 0, i))
    kern = functools.partial(_peer_mix_kernel, eb=eb)
    return pl.pallas_call(
        kern,
        out_shape=jax.ShapeDtypeStruct((t, D_MODEL), F32),
        grid=(t // tt, N_EXPERTS // eb),
        in_specs=[pl.BlockSpec((tt, D_MODEL), lambda i, e: (i, 0)),
                  pl.BlockSpec((D_MODEL, tt), lambda i, e: (0, i)),
                  pl.BlockSpec((eb, D_MODEL), lambda i, e: (e, 0)),
                  pl.BlockSpec((eb // PEER_SUB, D_MODEL, PEER_SUB), lambda i, e: (e, 0, 0)),
                  per_head, per_head, per_head, per_head],
        out_specs=pl.BlockSpec((tt, D_MODEL), lambda i, e: (i, 0)),
        scratch_shapes=[pltpu.VMEM((D_MODEL, tt), F32), pltpu.VMEM((2, PEER_SUB, tt), F32),
                        pltpu.VMEM((2, PEER_SUB, tt), BF16),
                        pltpu.VMEM((PK_HEADS, N_KEYS, tt), BF16), pltpu.VMEM((PK_HEADS, N_KEYS, tt), BF16)],
        compiler_params=_params(("parallel", "arbitrary")),
        name="peer_mix",
    )(x, h, u, v_sub, e1, cnt, e2, r2)


def _rope_tables(pos):
    half = HEAD_DIM // 2
    inv_freq = ROPE_THETA ** (-jnp.arange(half, dtype=F32) / half)
    ang = pos[:, None] * inv_freq[None, :]
    cos, sin = jnp.cos(ang), jnp.sin(ang)
    reps = LANES // HEAD_DIM
    return jnp.tile(jnp.concatenate([cos, cos], -1), (1, reps)), jnp.tile(jnp.concatenate([-sin, sin], -1), (1, reps))


def _lambda_init(layer):
    return 0.8 - 0.6 * math.exp(-0.3 * layer)


def _pad_lanes(v, width=LANES):
    return jnp.pad(v, (0, width - v.shape[0])).reshape(1, width)


def kernel(x_prompt, x_sample, cache_k, cache_v, state_ssm, state_conv, norm_mix, w_in, q_norm, k_norm, lam_q1, lam_k1, lam_q2, lam_k2, attn_subln, conv_w, conv_b, dt_bias, a_log, d_skip, ssm_norm, w_proj_attn, w_proj_ssm, w_out, norm_ffn, peer_wq, peer_k1, peer_k2, peer_u, peer_v):
    bp, lp, _ = x_prompt.shape
    bs, ls, _ = x_sample.shape
    depth = w_in.shape[0]
    n_past = cache_k.shape[2]
    tp, ts = bp * lp, bs * ls
    x = jnp.concatenate([x_prompt.reshape(tp, D_MODEL), x_sample.reshape(ts, D_MODEL)], axis=0)
    t_all = tp + ts
    tm = math.gcd(t_all, 1024)
    tt = math.gcd(t_all, 256)

    cos_p, sin_p = _rope_tables(jnp.tile(jnp.arange(lp, dtype=F32), bp))
    cos_s, sin_s = _rope_tables(jnp.tile(n_past + jnp.arange(ls, dtype=F32), bs))
    cache_kt = jnp.transpose(cache_k, (0, 1, 3, 4, 5, 2)).reshape(depth, bs, ATT_DIM, n_past)
    cache_v = cache_v.reshape(depth, bs, n_past * N_HEADS, 2 * HEAD_DIM)
    conv0_p = jnp.zeros((bp, CONV_W - 1, CONV_DIM), F32)
    ssm0_p = jnp.zeros((bp, SSM_HEADS, SSM_HEAD_DIM, D_STATE), F32)

    outs = {name: [] for name in ("kp", "vp", "sp", "cp", "ks", "vs", "ss", "cs")}
    for i in range(depth):
        li = _lambda_init(i)
        w_i = w_in[i].astype(BF16)
        w_qkv = w_i[:, :3 * ATT_DIM]
        w_rest = jnp.zeros((D_MODEL, PROJ_COLS), BF16)
        for col, piece in ((COL_Z, w_i[:, 3072:5120]), (COL_GATE, w_i[:, 9248:]), (COL_XBC, w_i[:, 5120:9216]),
                           (COL_DT, w_i[:, 9216:9248])):
            w_rest = lax.dynamic_update_slice(w_rest, piece, (0, col))
        gain_mix = norm_mix[i].reshape(1, D_MODEL)
        proj = in_proj(x, gain_mix, w_rest, tm=tm)

        q_gain = jnp.tile(q_norm[i], LANES // HEAD_DIM).reshape(1, LANES)
        k_gain = jnp.tile(k_norm[i], LANES // HEAD_DIM).reshape(1, LANES)
        qkv_w = (gain_mix, w_qkv)
        qb_p, k_p, kb_p, v_p, vb_p = qkv_proj(x, *qkv_w, cos_p, sin_p, q_gain, k_gain, row0=0, rows=tp,
                                              tm=math.gcd(tp, 512))
        qb_s, k_s, kb_s, v_s, vb_s = qkv_proj(x, *qkv_w, cos_s, sin_s, q_gain, k_gain, row0=tp, rows=ts,
                                              tm=math.gcd(math.gcd(tp, ts), 512))

        lam_rows = jnp.stack([lam_q1[i], lam_k1[i], lam_q2[i], lam_k2[i]])
        sub_gain = attn_subln[i].reshape(1, LANES)
        att_p = attn_prompt(qb_p, kb_p, vb_p, lam_rows, sub_gain, li, batch=bp, seq=lp,
                            tq=min(ATT_TQ, lp), tk=min(ATT_TK, lp), hps=ATT_HPS)
        att_s = attn_sample(qb_s, kb_s, vb_s, cache_kt, cache_v, lam_rows, sub_gain, li,
                            layer=i, batch=bs, rows=ls, tk=min(ATT_TKS, n_past))

        ssd_w = (conv_w[i], conv_b[i].reshape(1, CONV_DIM), _pad_lanes(dt_bias[i]), _pad_lanes(a_log[i]),
                 jnp.repeat(d_skip[i], SSM_HEAD_DIM).reshape(1, D_INNER), ssm_norm[i].reshape(1, D_INNER))
        y_p, ssm_p, conv_p = ssd(proj, conv0_p, ssm0_p, *ssd_w, batch=bp, seq=lp, row0=0, lc=min(CHUNK, lp))
        y_s, ssm_s, conv_s = ssd(proj, state_conv[i], state_ssm[i], *ssd_w, batch=bs, seq=ls, row0=tp,
                                 lc=min(CHUNK, ls))

        x = out_proj(x, att_p, att_s, y_p, y_s, proj, w_proj_attn[i].astype(BF16), w_proj_ssm[i].astype(BF16),
                     w_out[i].astype(BF16), tm=math.gcd(math.gcd(tp, ts), 512))

        h2, *tables = peer_keys(x, norm_ffn[i].reshape(1, D_MODEL), peer_wq[i].T.astype(BF16),
                                peer_k1[i].astype(BF16), peer_k2[i].astype(BF16), tt=tt)
        v_sub = peer_v[i].astype(BF16).reshape(N_EXPERTS // PEER_SUB, PEER_SUB, D_MODEL).transpose(0, 2, 1)
        x = peer_mix(x, h2, peer_u[i].astype(BF16), v_sub, *tables, tt=tt, eb=PEER_EB)

        outs["kp"].append(k_p.reshape(bp, lp, N_HEADS, 2, HEAD_DIM))
        outs["vp"].append(v_p.reshape(bp, lp, N_HEADS, 2 * HEAD_DIM))
        outs["sp"].append(ssm_p)
        outs["cp"].append(conv_p)
        outs["ks"].append(k_s.reshape(bs, ls, N_HEADS, 2, HEAD_DIM))
        outs["vs"].append(v_s.reshape(bs, ls, N_HEADS, 2 * HEAD_DIM))
        outs["ss"].append(ssm_s)
        outs["cs"].append(conv_s)

    stack = lambda name: jnp.stack(outs[name])
    return (x[:tp].reshape(bp, lp, D_MODEL), x[tp:].reshape(bs, ls, D_MODEL),
            stack("kp"), stack("vp"), stack("sp"), stack("cp"),
            stack("ks"), stack("vs"), stack("ss"), stack("cs"))
```

```python
import functools
import math

import jax
import jax.numpy as jnp
from jax import lax
from jax.experimental import pallas as pl
from jax.experimental.pallas import tpu as pltpu

F32 = jnp.float32
BF16 = jnp.bfloat16

D_MODEL = 1024
CHUNK = 64
N_HEADS = 8
HEAD_DIM = 64
ATT_DIM = N_HEADS * 2 * HEAD_DIM
ROPE_THETA = 10000.0
D_INNER = 2048
SSM_HEAD_DIM = 64
SSM_HEADS = D_INNER // SSM_HEAD_DIM
SSM_GROUPS = 8
HEADS_PER_GROUP = SSM_HEADS // SSM_GROUPS
D_STATE = 128
CONV_W = 4
CONV_DIM = D_INNER + 2 * SSM_GROUPS * D_STATE
PK_HEADS = 8
N_KEYS = 128
N_EXPERTS = N_KEYS * N_KEYS
PK_TOPK = 16
PK_HALF = 128
EPS = 1e-6

LANES = 128
SUBLANES = 8
V7X_VMEM_BYTES = 64 * 1024 * 1024
VMEM_LIMIT = V7X_VMEM_BYTES * 7 // 8

COL_Z = 0
COL_GATE = 2048
COL_XBC = 4096
COL_DT = 8192
PROJ_COLS = 8320
PROJ_TN = 1664
ATT_TQ = 512
ATT_TK = 512
ATT_TKS = 1024
ATT_HPS = 2
PEER_SUB = 256
PEER_EB = 8192

HIGHEST = lax.Precision.HIGHEST
NT_DIMS = (((1,), (1,)), ((), ()))


def _params(sem):
    return pltpu.CompilerParams(dimension_semantics=sem, vmem_limit_bytes=VMEM_LIMIT)


def _dot(a, b):
    return jnp.dot(a, b, preferred_element_type=F32)


def _dot_nt(a, b):
    return lax.dot_general(a, b, NT_DIMS, preferred_element_type=F32)


def _dot_exact(a, b):
    return jnp.dot(a, b, preferred_element_type=F32, precision=HIGHEST)


def _dot_split(a, b):
    head = a.astype(BF16)
    rest = (a - head.astype(F32)).astype(BF16)
    return _dot(head, b) + _dot(rest, b)


def _transpose_exact(x):
    c = x.shape[1]
    eye = (lax.broadcasted_iota(jnp.int32, (c, c), 0) == lax.broadcasted_iota(jnp.int32, (c, c), 1)).astype(F32)
    return lax.dot_general(eye, x, NT_DIMS, preferred_element_type=F32, precision=HIGHEST)


def _sigmoid(x):
    return 1.0 / (1.0 + jnp.exp(-x))


def _silu(x):
    return x * _sigmoid(x)


def _softplus(x):
    return jnp.maximum(x, 0.0) + jnp.log(1.0 + jnp.exp(-jnp.abs(x)))


def _in_proj_kernel(x_ref, g_ref, w_ref, o_ref, h_scr):
    @pl.when(pl.program_id(1) == 0)
    def _():
        x = x_ref[...]
        ms = jnp.mean(x * x, axis=-1, keepdims=True)
        h_scr[...] = (x * lax.rsqrt(ms + EPS) * g_ref[...]).astype(BF16)

    o_ref[...] = _dot(h_scr[...], w_ref[...])


def in_proj(x, gain, w, *, tm):
    t = x.shape[0]
    n = w.shape[1]
    return pl.pallas_call(
        _in_proj_kernel,
        out_shape=jax.ShapeDtypeStruct((t, n), F32),
        grid=(t // tm, n // PROJ_TN),
        in_specs=[pl.BlockSpec((tm, D_MODEL), lambda i, j: (i, 0)),
                  pl.BlockSpec((1, D_MODEL), lambda i, j: (0, 0)),
                  pl.BlockSpec((D_MODEL, PROJ_TN), lambda i, j: (0, j))],
        out_specs=pl.BlockSpec((tm, PROJ_TN), lambda i, j: (i, j)),
        scratch_shapes=[pltpu.VMEM((tm, D_MODEL), BF16)],
        compiler_params=_params(("parallel", "arbitrary")),
        name="in_proj",
    )(x, gain, w)


def _qkv_proj_kernel(x_ref, g_ref, w_ref, cos_ref, sin_ref, qg_ref, kg_ref,
                     qb_ref, ko_ref, kb_ref, vo_ref, vb_ref):
    x = x_ref[...]
    ms = jnp.mean(x * x, axis=-1, keepdims=True)
    h = (x * lax.rsqrt(ms + EPS) * g_ref[...]).astype(BF16)
    qkv = _dot(h, w_ref[...])
    cos = cos_ref[...]
    sin = sin_ref[...]
    row = lax.broadcasted_iota(jnp.int32, (LANES, LANES), 0) // HEAD_DIM
    col = lax.broadcasted_iota(jnp.int32, (LANES, LANES), 1) // HEAD_DIM
    group_mean = jnp.where(row == col, 1.0 / HEAD_DIM, 0.0).astype(BF16)
    lane = lax.broadcasted_iota(jnp.int32, (1, LANES), 1)
    low_half = (lane % HEAD_DIM) < (HEAD_DIM // 2)

    def norm_rope(y, gain):
        ms = _dot_split(y * y, group_mean)
        y = y * lax.rsqrt(ms + EPS) * gain
        partner = jnp.where(low_half, pltpu.roll(y, LANES - HEAD_DIM // 2, 1), pltpu.roll(y, HEAD_DIM // 2, 1))
        return y * cos + partner * sin

    for h_i in range(N_HEADS):
        sl = slice(h_i * LANES, (h_i + 1) * LANES)
        q = norm_rope(qkv[:, h_i * LANES:(h_i + 1) * LANES], qg_ref[...]) * (HEAD_DIM ** -0.5)
        qb_ref[:, sl] = q.astype(BF16)
        k = norm_rope(qkv[:, ATT_DIM + h_i * LANES:ATT_DIM + (h_i + 1) * LANES], kg_ref[...])
        ko_ref[:, sl] = k
        kb_ref[:, sl] = k.astype(BF16)
    v = qkv[:, 2 * ATT_DIM:]
    vo_ref[...] = v
    vb_ref[...] = v.astype(BF16)


def qkv_proj(x, gain, w_qkv, cos, sin, q_gain, k_gain, *, row0, rows, tm):
    t = rows
    tile0 = row0 // tm
    tab = pl.BlockSpec((tm, LANES), lambda i: (i, 0))
    vec = pl.BlockSpec((1, LANES), lambda i: (0, 0))
    out = pl.BlockSpec((tm, ATT_DIM), lambda i: (i, 0))
    return pl.pallas_call(
        _qkv_proj_kernel,
        out_shape=[jax.ShapeDtypeStruct((t, ATT_DIM), BF16),
                   jax.ShapeDtypeStruct((t, ATT_DIM), F32),
                   jax.ShapeDtypeStruct((t, ATT_DIM), BF16),
                   jax.ShapeDtypeStruct((t, ATT_DIM), F32),
                   jax.ShapeDtypeStruct((t, ATT_DIM), BF16)],
        grid=(t // tm,),
        in_specs=[pl.BlockSpec((tm, D_MODEL), lambda i: (tile0 + i, 0)),
                  pl.BlockSpec((1, D_MODEL), lambda i: (0, 0)),
                  pl.BlockSpec((D_MODEL, 3 * ATT_DIM), lambda i: (0, 0)),
                  tab, tab, vec, vec],
        out_specs=[out, out, out, out, out],
        compiler_params=_params(("parallel",)),
        name="qkv_proj",
    )(x, gain, w_qkv, cos, sin, q_gain, k_gain)


def _lambda_full(lam_ref, lam_init):
    a = jnp.sum(lam_ref[0:1, :] * lam_ref[1:2, :], axis=-1, keepdims=True)
    b = jnp.sum(lam_ref[2:3, :] * lam_ref[3:4, :], axis=-1, keepdims=True)
    return jnp.exp(a) - jnp.exp(b) + lam_init


def _split_components(q):
    lane = lax.broadcasted_iota(jnp.int32, (1, LANES), 1)
    zero = jnp.zeros_like(q)
    return jnp.concatenate([jnp.where(lane < HEAD_DIM, q, zero), jnp.where(lane >= HEAD_DIM, q, zero)], axis=0)


def _diff_combine(acc, l, rows, lam, gain, lam_init):
    o1 = acc[:rows] / l[:rows]
    o2 = acc[rows:] / l[rows:]
    o = o1 - lam * o2
    ms = jnp.mean(o * o, axis=-1, keepdims=True)
    return o * lax.rsqrt(ms + EPS) * gain * (1.0 - lam_init)


def _attn_prompt_kernel(lam_ref, g_ref, q_ref, k_ref, v_ref, o_ref,
                        vext_scr, s_scr, p_scr, m_scr, alpha_scr, acc_scr, *, tq, tk, hps, lam_init):
    i = pl.program_id(2)
    rows = 2 * tq
    heads = range(hps)

    @pl.when(i == 0)
    def _():
        for hh in heads:
            vext_scr[hh, :, :LANES] = v_ref[:, hh * LANES:(hh + 1) * LANES]
            vext_scr[hh, :, LANES:] = jnp.ones((vext_scr.shape[1], LANES), BF16)

    qq = [_split_components(q_ref[:, hh * LANES:(hh + 1) * LANES]) for hh in heads]
    diag = (i * tq) // tk
    n_blocks = diag + 1

    def key_block(t):
        return jnp.where(t == 0, diag, t - 1)

    def scores(hh, blk):
        return _dot_nt(qq[hh], k_ref[pl.ds(pl.multiple_of(blk * tk, tk), tk), hh * LANES:(hh + 1) * LANES])

    def softmax(hh, slot, bias=None):
        s = s_scr[hh, slot]
        if bias is not None:
            s = s + bias
        m_old = m_scr[hh]
        m_new = jnp.maximum(m_old, jnp.max(s, axis=-1, keepdims=True))
        alpha_scr[hh, slot] = jnp.exp(m_old - m_new)
        for c in range(tk // LANES):
            sl = slice(c * LANES, (c + 1) * LANES)
            p_scr[hh, slot, :, sl] = jnp.exp(s[:, sl] - m_new).astype(BF16)
        m_scr[hh] = m_new

    def accumulate(hh, slot, blk):
        pv = _dot(p_scr[hh, slot], vext_scr[hh, pl.ds(pl.multiple_of(blk * tk, tk), tk), :])
        alpha = alpha_scr[hh, slot]
        acc_scr[hh, :, :LANES] = alpha * acc_scr[hh, :, :LANES] + pv[:, :LANES]
        acc_scr[hh, :, LANES:] = alpha * acc_scr[hh, :, LANES:] + pv[:, LANES:]

    m_scr[...] = jnp.full(m_scr.shape, -jnp.inf, F32)
    acc_scr[...] = jnp.zeros(acc_scr.shape, F32)
    q_chunk = (i * tq + lax.broadcasted_iota(jnp.int32, (rows, 1), 0) % tq) // CHUNK
    k_chunk = (diag * tk + lax.broadcasted_iota(jnp.int32, (1, tk), 1)) // CHUNK
    for hh in heads:
        p_scr[hh, 1] = jnp.zeros(p_scr.shape[2:], BF16)
        alpha_scr[hh, 1] = jnp.ones(alpha_scr.shape[2:], F32)
        s_scr[hh, 0] = jnp.where(k_chunk <= q_chunk, scores(hh, diag), -jnp.inf)

    last = n_blocks - 1

    def body(pair, carry):
        t = 2 * pair
        for hh in heads:
            s_scr[hh, 1] = scores(hh, key_block(jnp.minimum(t + 1, last)))
            softmax(hh, 0)
            accumulate(hh, 1, key_block(jnp.maximum(t - 1, 0)))
        for hh in heads:
            s_scr[hh, 0] = scores(hh, key_block(jnp.minimum(t + 2, last)))
            softmax(hh, 1, jnp.where(t + 1 <= last, 0.0, -jnp.inf))
            accumulate(hh, 0, key_block(t))
        return carry

    n_pairs = (n_blocks + 1) // 2
    lax.fori_loop(0, n_pairs, body, 0)
    lam = _lambda_full(lam_ref, lam_init)
    for hh in heads:
        accumulate(hh, 1, key_block(jnp.minimum(2 * n_pairs - 1, last)))
        acc = acc_scr[hh]
        o_ref[:, hh * LANES:(hh + 1) * LANES] = _diff_combine(
            acc[:, :LANES], acc[:, LANES:], tq, lam, g_ref[...], lam_init).astype(o_ref.dtype)


def attn_prompt(qb, kb, vb, lam_rows, sub_gain, lam_init, *, batch, seq, tq, tk, hps):
    nq = seq // tq
    assert tk % tq == 0 and tq % CHUNK == 0, "one key block must hold all chunks of a query block"
    assert N_HEADS % hps == 0
    kern = functools.partial(_attn_prompt_kernel, tq=tq, tk=tk, hps=hps, lam_init=lam_init)
    width = hps * LANES
    return pl.pallas_call(
        kern,
        out_shape=jax.ShapeDtypeStruct((batch * seq, ATT_DIM), BF16),
        grid=(batch, N_HEADS // hps, nq),
        in_specs=[pl.BlockSpec((4, HEAD_DIM), lambda b, h, i: (0, 0)),
                  pl.BlockSpec((1, LANES), lambda b, h, i: (0, 0)),
                  pl.BlockSpec((tq, width), lambda b, h, i: (b * nq + i, h)),
                  pl.BlockSpec((seq, width), lambda b, h, i: (b, h)),
                  pl.BlockSpec((seq, width), lambda b, h, i: (b, h))],
        out_specs=pl.BlockSpec((tq, width), lambda b, h, i: (b * nq + i, h)),
        scratch_shapes=[pltpu.VMEM((hps, seq, 2 * LANES), BF16),
                        pltpu.VMEM((hps, 2, 2 * tq, tk), F32), pltpu.VMEM((hps, 2, 2 * tq, tk), BF16),
                        pltpu.VMEM((hps, 2 * tq, LANES), F32), pltpu.VMEM((hps, 2, 2 * tq, LANES), F32),
                        pltpu.VMEM((hps, 2 * tq, 2 * LANES), F32)],
        compiler_params=_params(("parallel", "parallel", "arbitrary")),
        name="attn_prompt",
    )(lam_rows, sub_gain, qb, kb, vb)


def _attn_sample_kernel(lam_ref, g_ref, q_ref, kn_ref, vn_ref, kt_ref, vc_ref, o_ref,
                        qq_scr, m_scr, l_scr, acc_scr, *, rows, tk, lam_init):
    j = pl.program_id(1)
    hr = 2 * rows

    @pl.when(j == 0)
    def _():
        for h in range(N_HEADS):
            qq_scr[h * hr:(h + 1) * hr, :] = _split_components(q_ref[:, h * LANES:(h + 1) * LANES])
        m_scr[...] = jnp.full(m_scr.shape, -jnp.inf, F32)
        l_scr[...] = jnp.zeros(l_scr.shape, F32)
        acc_scr[...] = jnp.zeros(acc_scr.shape, F32)

    def update(scores_of, values_of):
        s = jnp.concatenate([scores_of(h, qq_scr[h * hr:(h + 1) * hr, :]) for h in range(N_HEADS)], axis=0)
        m_old = m_scr[...]
        m_new = jnp.maximum(m_old, jnp.max(s, axis=-1, keepdims=True))
        alpha = jnp.exp(m_old - m_new)
        p = jnp.exp(s - m_new)
        l_scr[...] = alpha * l_scr[...] + jnp.sum(p, axis=-1, keepdims=True)
        m_scr[...] = m_new
        p = p.astype(BF16)
        for h in range(N_HEADS):
            hs = slice(h * hr, (h + 1) * hr)
            acc_scr[hs, :] = alpha[hs] * acc_scr[hs, :] + _dot(p[hs], values_of(h))

    update(lambda h, qq: _dot(qq, kt_ref[h * LANES:(h + 1) * LANES, :].astype(BF16)),
           lambda h: vc_ref[pl.ds(h, tk, stride=N_HEADS), :].astype(BF16))

    @pl.when(j == pl.num_programs(1) - 1)
    def _():
        update(lambda h, qq: _dot_nt(qq, kn_ref[:, h * LANES:(h + 1) * LANES]),
               lambda h: vn_ref[:, h * LANES:(h + 1) * LANES])
        lam = _lambda_full(lam_ref, lam_init)
        for h in range(N_HEADS):
            hs = slice(h * hr, (h + 1) * hr)
            o_ref[:, h * LANES:(h + 1) * LANES] = _diff_combine(
                acc_scr[hs, :], l_scr[hs, :], rows, lam, g_ref[...], lam_init).astype(o_ref.dtype)


def attn_sample(qb, kb, vb, cache_kt, cache_v, lam_rows, sub_gain, lam_init, *, layer, batch, rows, tk):
    past = cache_kt.shape[3]
    new_blk = pl.BlockSpec((rows, ATT_DIM), lambda b, j: (b, 0))
    kern = functools.partial(_attn_sample_kernel, rows=rows, tk=tk, lam_init=lam_init)
    return pl.pallas_call(
        kern,
        out_shape=jax.ShapeDtypeStruct((batch * rows, ATT_DIM), BF16),
        grid=(batch, past // tk),
        in_specs=[pl.BlockSpec((4, HEAD_DIM), lambda b, j: (0, 0)),
                  pl.BlockSpec((1, LANES), lambda b, j: (0, 0)),
                  new_blk, new_blk, new_blk,
                  pl.BlockSpec((None, None, ATT_DIM, tk), lambda b, j: (layer, b, 0, j)),
                  pl.BlockSpec((None, None, tk * N_HEADS, 2 * HEAD_DIM), lambda b, j: (layer, b, j, 0))],
        out_specs=pl.BlockSpec((rows, ATT_DIM), lambda b, j: (b, 0)),
        scratch_shapes=[pltpu.VMEM((N_HEADS * 2 * rows, LANES), BF16),
                        pltpu.VMEM((N_HEADS * 2 * rows, 1), F32), pltpu.VMEM((N_HEADS * 2 * rows, 1), F32),
                        pltpu.VMEM((N_HEADS * 2 * rows, LANES), F32)],
        compiler_params=_params(("parallel", "arbitrary")),
        name="attn_sample",
    )(lam_rows, sub_gain, qb, kb, vb, cache_kt, cache_v)


def _ssd_kernel(xbc_ref, z_ref, dt_ref, conv0_ref, ssm0_ref, cw_ref, cb_ref, dtb_ref, alog_ref, dskip_ref, ng_ref,
                y_ref, ssm_ref, conv_ref, xpad_scr, *, lc):
    c = pl.program_id(1)
    pad0 = SUBLANES - (CONV_W - 1)

    @pl.when(c == 0)
    def _():
        xpad_scr[pad0:SUBLANES, :] = conv0_ref[...]
        ssm_ref[...] = ssm0_ref[...]

    xpad_scr[SUBLANES:, :] = xbc_ref[...]
    conv = cb_ref[...] + sum(xpad_scr[pad0 + j:pad0 + j + lc, :] * cw_ref[j:j + 1, :] for j in range(CONV_W))
    conv = _silu(conv)
    tail = xpad_scr[lc + pad0:lc + SUBLANES, :]
    xpad_scr[pad0:SUBLANES, :] = tail
    conv_ref[...] = tail

    dt = _softplus(dt_ref[...] + dtb_ref[...])
    da = dt * (-jnp.exp(alog_ref[...]))
    row = lax.broadcasted_iota(jnp.int32, (lc, lc), 0)
    col = lax.broadcasted_iota(jnp.int32, (lc, lc), 1)
    causal = row >= col
    a_cum = _dot_exact(causal.astype(F32), da)
    dt_t = _transpose_exact(dt)
    a_cum_t = _transpose_exact(a_cum)
    eye = (lax.broadcasted_iota(jnp.int32, (LANES, LANES), 0)
           == lax.broadcasted_iota(jnp.int32, (LANES, LANES), 1)).astype(BF16)

    for g in range(SSM_GROUPS):
        b_g = conv[:, D_INNER + g * D_STATE:D_INNER + (g + 1) * D_STATE].astype(BF16)
        c_off = D_INNER + SSM_GROUPS * D_STATE
        c_g = conv[:, c_off + g * D_STATE:c_off + (g + 1) * D_STATE].astype(BF16)
        cb = _dot_nt(c_g, b_g)
        ys = []
        for pair in range(HEADS_PER_GROUP // 2):
            lane0 = (g * HEADS_PER_GROUP + 2 * pair) * SSM_HEAD_DIM
            x_pair = conv[:, lane0:lane0 + LANES].astype(BF16)
            x_pair_t = _dot_nt(eye, x_pair)
            for sub in range(2):
                h = g * HEADS_PER_GROUP + 2 * pair + sub
                x_h = x_pair[:, sub * SSM_HEAD_DIM:(sub + 1) * SSM_HEAD_DIM]
                x_h_t = x_pair_t[sub * SSM_HEAD_DIM:(sub + 1) * SSM_HEAD_DIM, :]
                a_col = a_cum[:, h:h + 1]
                a_row = a_cum_t[h:h + 1, :]
                dt_row = dt_t[h:h + 1, :]
                a_last = a_cum_t[h:h + 1, lc - 1:lc]
                decay = jnp.exp(jnp.where(causal, a_col - a_row, -jnp.inf))
                w_in = (cb * decay * dt_row).astype(BF16)
                state = ssm_ref[h]
                y_h = _dot(w_in, x_h) + jnp.exp(a_col) * _dot_nt(c_g, state.astype(BF16))
                ys.append(y_h)
                w_state = (x_h_t * (dt_row * jnp.exp(a_last - a_row))).astype(BF16)
                ssm_ref[h] = jnp.exp(a_last) * state + _dot(w_state, b_g)
        lanes = slice(g * HEADS_PER_GROUP * SSM_HEAD_DIM, (g + 1) * HEADS_PER_GROUP * SSM_HEAD_DIM)
        y_g = jnp.concatenate(ys, axis=-1) + dskip_ref[:, lanes] * conv[:, lanes]
        y_g = y_g * _silu(z_ref[:, lanes])
        ms = jnp.mean(y_g * y_g, axis=-1, keepdims=True)
        y_ref[:, lanes] = (y_g * lax.rsqrt(ms + EPS) * ng_ref[:, lanes]).astype(y_ref.dtype)


def ssd(proj, conv0, ssm0, conv_w, conv_b, dt_bias, a_log, d_skip, norm_gain, *, batch, seq, row0, lc):
    nc = seq // lc
    tok = lambda width, col: pl.BlockSpec((lc, width), lambda b, c: (row0 // lc + b * nc + c, col // width))
    const = lambda r, w: pl.BlockSpec((r, w), lambda b, c: (0, 0))
    kern = functools.partial(_ssd_kernel, lc=lc)
    return pl.pallas_call(
        kern,
        out_shape=[jax.ShapeDtypeStruct((batch * seq, D_INNER), BF16),
                   jax.ShapeDtypeStruct((batch, SSM_HEADS, SSM_HEAD_DIM, D_STATE), F32),
                   jax.ShapeDtypeStruct((batch, CONV_W - 1, CONV_DIM), F32)],
        grid=(batch, nc),
        in_specs=[tok(CONV_DIM, COL_XBC), tok(D_INNER, COL_Z), tok(LANES, COL_DT),
                  pl.BlockSpec((None, CONV_W - 1, CONV_DIM), lambda b, c: (b, 0, 0)),
                  pl.BlockSpec((None, SSM_HEADS, SSM_HEAD_DIM, D_STATE), lambda b, c: (b, 0, 0, 0)),
                  const(CONV_W, CONV_DIM), const(1, CONV_DIM), const(1, LANES), const(1, LANES),
                  const(1, D_INNER), const(1, D_INNER)],
        out_specs=[pl.BlockSpec((lc, D_INNER), lambda b, c: (b * nc + c, 0)),
                   pl.BlockSpec((None, SSM_HEADS, SSM_HEAD_DIM, D_STATE), lambda b, c: (b, 0, 0, 0)),
                   pl.BlockSpec((None, CONV_W - 1, CONV_DIM), lambda b, c: (b, 0, 0))],
        scratch_shapes=[pltpu.VMEM((SUBLANES + lc, CONV_DIM), F32)],
        compiler_params=_params(("parallel", "arbitrary")),
        name="ssd",
    )(proj, proj, proj, conv0, ssm0, conv_w, conv_b, dt_bias, a_log, d_skip, norm_gain)


def _out_proj_kernel(x_ref, attp_ref, atts_ref, yp_ref, ys_ref, ga_ref, gm_ref, wa_ref, wm_ref, wo_ref, o_ref, *,
                     prompt_tiles):
    is_prompt = pl.program_id(0) < prompt_tiles
    att = jnp.where(is_prompt, attp_ref[...], atts_ref[...])
    y = jnp.where(is_prompt, yp_ref[...], ys_ref[...])
    merged = (_sigmoid(ga_ref[...]) * _dot(att, wa_ref[...]) + _sigmoid(gm_ref[...]) * _dot(y, wm_ref[...]))
    o_ref[...] = x_ref[...] + _dot(merged.astype(BF16), wo_ref[...])


def out_proj(x, att_p, att_s, y_p, y_s, proj, wa, wm, wo, *, tm):
    t = x.shape[0]
    n_p, n_s = att_p.shape[0] // tm, att_s.shape[0] // tm
    assert n_p * tm == att_p.shape[0] and n_s * tm == att_s.shape[0] and n_p + n_s == t // tm
    gate = lambda c: pl.BlockSpec((tm, D_MODEL), lambda i, c=c: (i, c // D_MODEL))
    const = lambda r: pl.BlockSpec((r, D_MODEL), lambda i: (0, 0))
    prompt = lambda w: pl.BlockSpec((tm, w), lambda i: (jnp.minimum(i, n_p - 1), 0))
    sample = lambda w: pl.BlockSpec((tm, w), lambda i: (jnp.maximum(i - n_p, 0), 0))
    return pl.pallas_call(
        functools.partial(_out_proj_kernel, prompt_tiles=n_p),
        out_shape=jax.ShapeDtypeStruct((t, D_MODEL), F32),
        grid=(t // tm,),
        in_specs=[pl.BlockSpec((tm, D_MODEL), lambda i: (i, 0)),
                  prompt(ATT_DIM), sample(ATT_DIM), prompt(D_INNER), sample(D_INNER),
                  gate(COL_GATE), gate(COL_GATE + D_MODEL),
                  const(ATT_DIM), const(D_INNER), const(D_MODEL)],
        out_specs=pl.BlockSpec((tm, D_MODEL), lambda i: (i, 0)),
        compiler_params=_params(("parallel",)),
        name="out_proj",
    )(x, att_p, att_s, y_p, y_s, proj, proj, wa, wm, wo)


def _top_values(s, k):
    vals = []
    for _ in range(k):
        m = jnp.max(s, axis=0, keepdims=True)
        vals.append(m)
        s = jnp.where(s == m, -jnp.inf, s)
    return vals


def _top_ranked(s, k):
    vals = []
    rank = jnp.full(s.shape, float(k), F32)
    for i in range(k):
        m = jnp.max(s, axis=0, keepdims=True)
        hit = s == m
        vals.append(m)
        rank = jnp.where(hit, float(i), rank)
        s = jnp.where(hit, -jnp.inf, s)
    return vals, rank


def _peer_keys_kernel(x_ref, g_ref, wqt_ref, k1_ref, k2_ref, ht_ref, e1_ref, cnt_ref, e2_ref, r2_ref):
    x = x_ref[...]
    ms = jnp.mean(x * x, axis=-1, keepdims=True)
    h32 = x * lax.rsqrt(ms + EPS) * g_ref[...]
    h_t = h32.T.astype(BF16)
    ht_ref[...] = h_t
    q_t = _dot(wqt_ref[...], h_t)
    for hd in range(PK_HEADS):
        r0 = hd * 2 * PK_HALF
        s1 = _dot(k1_ref[hd], q_t[r0:r0 + PK_HALF, :].astype(BF16))
        s2 = _dot(k2_ref[hd], q_t[r0 + PK_HALF:r0 + 2 * PK_HALF, :].astype(BF16))
        v1 = _top_values(s1, PK_TOPK)
        v2, r2 = _top_ranked(s2, PK_TOPK)
        widths = [PK_TOPK // (a + 1) for a in range(PK_TOPK)]
        pad_rows = -sum(widths) % SUBLANES
        cand = jnp.concatenate([v1[a] + jnp.concatenate(v2[:widths[a]], axis=0) for a in range(PK_TOPK)]
                               + [jnp.full((pad_rows, s1.shape[1]), -jnp.inf, F32)], axis=0)
        tau = _top_values(cand, PK_TOPK)[-1]
        top = v1[0] + v2[0]
        chosen = cand >= tau
        z = jnp.sum(jnp.where(chosen, jnp.exp(cand - top), 0.0), axis=0, keepdims=True)
        cnt = jnp.zeros(s1.shape, F32)
        row0 = 0
        for a in range(PK_TOPK):
            n_a = jnp.sum(chosen[row0:row0 + widths[a]].astype(F32), axis=0, keepdims=True)
            cnt = jnp.where(s1 == v1[a], n_a, cnt)
            row0 += widths[a]
        e1_ref[hd] = jnp.exp(s1 - v1[0]) / z
        cnt_ref[hd] = cnt
        e2_ref[hd] = jnp.exp(s2 - v2[0])
        r2_ref[hd] = r2


def peer_keys(x, gain, wq_t, k1, k2, *, tt):
    t = x.shape[0]
    per_head = pl.BlockSpec((PK_HEADS, N_KEYS, tt), lambda i: (0, 0, i))
    table = jax.ShapeDtypeStruct((PK_HEADS, N_KEYS, t), F32)
    return pl.pallas_call(
        _peer_keys_kernel,
        out_shape=[jax.ShapeDtypeStruct((D_MODEL, t), BF16), table, table, table, table],
        grid=(t // tt,),
        in_specs=[pl.BlockSpec((tt, D_MODEL), lambda i: (i, 0)),
                  pl.BlockSpec((1, D_MODEL), lambda i: (0, 0)),
                  pl.BlockSpec((PK_HEADS * 2 * PK_HALF, D_MODEL), lambda i: (0, 0)),
                  pl.BlockSpec((PK_HEADS, N_KEYS, PK_HALF), lambda i: (0, 0, 0)),
                  pl.BlockSpec((PK_HEADS, N_KEYS, PK_HALF), lambda i: (0, 0, 0))],
        out_specs=[pl.BlockSpec((D_MODEL, tt), lambda i: (0, i)), per_head, per_head, per_head, per_head],
        compiler_params=_params(("parallel",)),
        name="peer_keys",
    )(x, gain, wq_t, k1, k2)


def _gelu(x):
    return 0.5 * x * (1.0 + lax.erf(x * (2.0 ** -0.5)))


def _peer_mix_kernel(*refs, block, eb, first, last):
    refs = list(refs)
    h_ref, u_ref, vt_ref, e1_ref, cnt_ref, e2_ref, r2_ref = refs[:7]
    del refs[:7]
    accin_ref = None if first else refs.pop(0)
    x_ref = refs.pop(0) if last else None
    acc_ref = refs.pop(0)
    o_ref = refs.pop(0) if last else None
    raw_scr, coef_scr, e2_scr, r2_scr = refs
    tt = acc_ref.shape[1]
    n_sub = eb // PEER_SUB
    rows_per_sub = PEER_SUB // N_KEYS
    pack = 2 * SUBLANES

    acc_ref[...] = jnp.zeros(acc_ref.shape, F32) if first else accin_ref[...]
    for hd in range(PK_HEADS):
        e2_scr[hd] = e2_ref[hd].astype(BF16)
        r2_scr[hd] = r2_ref[hd].astype(BF16)

    def pre_activations(j):
        start = pl.multiple_of(j * PEER_SUB, PEER_SUB)
        return _dot(u_ref[pl.ds(start, PEER_SUB), :], h_ref[...])

    def mix(slot, j):
        zero = jnp.zeros((pack, LANES), BF16)
        for r in range(rows_per_sub):
            i1 = (block * n_sub + j) * rows_per_sub + r
            cnt_rows = [cnt_ref[hd, pl.ds(i1, 1), :] for hd in range(PK_HEADS)]
            e1_rows = [e1_ref[hd, pl.ds(i1, 1), :] for hd in range(PK_HEADS)]
            for lt in range(tt // LANES):
                lanes = slice(lt * LANES, (lt + 1) * LANES)
                cnt_b = [jnp.broadcast_to(row[:, lanes], (pack, LANES)).astype(BF16) for row in cnt_rows]
                e1_b = [jnp.broadcast_to(row[:, lanes], (pack, LANES)).astype(BF16) for row in e1_rows]
                for c in range(N_KEYS // pack):
                    keys = slice(c * pack, (c + 1) * pack)
                    gate = zero
                    for hd in range(PK_HEADS):
                        keep = r2_scr[hd, keys, lanes] < cnt_b[hd]
                        gate = gate + jnp.where(keep, e2_scr[hd, keys, lanes] * e1_b[hd], zero)
                    rows = slice(r * N_KEYS + c * pack, r * N_KEYS + (c + 1) * pack)
                    coef_scr[slot, rows, lanes] = gate * _gelu(raw_scr[slot, rows, lanes]).astype(BF16)

    def accumulate(slot, j):
        acc_ref[...] += _dot(vt_ref[j], coef_scr[slot])

    raw_scr[0] = pre_activations(0)
    coef_scr[1] = jnp.zeros(coef_scr.shape[1:], BF16)

    def body(pair, carry):
        j = 2 * pair
        raw_scr[1] = pre_activations(j + 1)
        mix(0, j)
        accumulate(1, jnp.maximum(j - 1, 0))
        raw_scr[0] = pre_activations(jnp.minimum(j + 2, n_sub - 1))
        mix(1, j + 1)
        accumulate(0, j)
        return carry

    lax.fori_loop(0, n_sub // 2, body, 0)
    accumulate(1, n_sub - 1)
    if last:
        o_ref[...] = x_ref[...] + acc_ref[...].T


def peer_mix(x, h_t, u, v_sub, e1, cnt, e2, r2, *, tt, eb):
    t = x.shape[0]
    n_blocks = N_EXPERTS // eb
    assert (eb // PEER_SUB) % 2 == 0
    per_head = pl.BlockSpec((PK_HEADS, N_KEYS, tt), lambda i: (0, 0, i))
    acc_blk = pl.BlockSpec((D_MODEL, tt), lambda i: (0, i))
    tok_blk = pl.BlockSpec((tt, D_MODEL), lambda i: (i, 0))
    acc_shape = jax.ShapeDtypeStruct((D_MODEL, t), F32)
    acc = None
    for block in range(n_blocks):
        first, last = block == 0, block == n_blocks - 1
        args = [h_t, u, v_sub, e1, cnt, e2, r2] + ([] if first else [acc]) + ([x] if last else [])
        in_specs = ([acc_blk,
                     pl.BlockSpec((eb, D_MODEL), lambda i, block=block: (block, 0), pipeline_mode=pl.Buffered(1)),
                     pl.BlockSpec((eb // PEER_SUB, D_MODEL, PEER_SUB), lambda i, block=block: (block, 0, 0),
                                  pipeline_mode=pl.Buffered(1)),
                     per_head, per_head, per_head, per_head]
                    + ([] if first else [acc_blk]) + ([tok_blk] if last else []))
        result = pl.pallas_call(
            functools.partial(_peer_mix_kernel, block=block, eb=eb, first=first, last=last),
            out_shape=[acc_shape] + ([jax.ShapeDtypeStruct((t, D_MODEL), F32)] if last else []),
            grid=(t // tt,),
            in_specs=in_specs,
            out_specs=[acc_blk] + ([tok_blk] if last else []),
            scratch_shapes=[pltpu.VMEM((2, PEER_SUB, tt), F32), pltpu.VMEM((2, PEER_SUB, tt), BF16),
                            pltpu.VMEM((PK_HEADS, N_KEYS, tt), BF16), pltpu.VMEM((PK_HEADS, N_KEYS, tt), BF16)],
            input_output_aliases={} if first else {7: 0},
            compiler_params=_params(("parallel",)),
            name="peer_mix",
        )(*args)
        acc = result[0]
    return result[1]


def _rope_tables(pos):
    half = HEAD_DIM // 2
    inv_freq = ROPE_THETA ** (-jnp.arange(half, dtype=F32) / half)
    ang = pos[:, None] * inv_freq[None, :]
    cos, sin = jnp.cos(ang), jnp.sin(ang)
    reps = LANES // HEAD_DIM
    return jnp.tile(jnp.concatenate([cos, cos], -1), (1, reps)), jnp.tile(jnp.concatenate([-sin, sin], -1), (1, reps))


def _lambda_init(layer):
    return 0.8 - 0.6 * math.exp(-0.3 * layer)


def _pad_lanes(v, width=LANES):
    return jnp.pad(v, (0, width - v.shape[0])).reshape(1, width)


def kernel(x_prompt, x_sample, cache_k, cache_v, state_ssm, state_conv, norm_mix, w_in, q_norm, k_norm, lam_q1, lam_k1, lam_q2, lam_k2, attn_subln, conv_w, conv_b, dt_bias, a_log, d_skip, ssm_norm, w_proj_attn, w_proj_ssm, w_out, norm_ffn, peer_wq, peer_k1, peer_k2, peer_u, peer_v):
    bp, lp, _ = x_prompt.shape
    bs, ls, _ = x_sample.shape
    depth = w_in.shape[0]
    n_past = cache_k.shape[2]
    tp, ts = bp * lp, bs * ls
    x = jnp.concatenate([x_prompt.reshape(tp, D_MODEL), x_sample.reshape(ts, D_MODEL)], axis=0)
    t_all = tp + ts
    tm = math.gcd(t_all, 1024)
    tt = math.gcd(t_all, 256)

    cos_p, sin_p = _rope_tables(jnp.tile(jnp.arange(lp, dtype=F32), bp))
    cos_s, sin_s = _rope_tables(jnp.tile(n_past + jnp.arange(ls, dtype=F32), bs))
    cache_kt = jnp.transpose(cache_k, (0, 1, 3, 4, 5, 2)).reshape(depth, bs, ATT_DIM, n_past)
    cache_v = cache_v.reshape(depth, bs, n_past * N_HEADS, 2 * HEAD_DIM)
    conv0_p = jnp.zeros((bp, CONV_W - 1, CONV_DIM), F32)
    ssm0_p = jnp.zeros((bp, SSM_HEADS, SSM_HEAD_DIM, D_STATE), F32)

    outs = {name: [] for name in ("kp", "vp", "sp", "cp", "ks", "vs", "ss", "cs")}
    for i in range(depth):
        li = _lambda_init(i)
        w_i = w_in[i].astype(BF16)
        w_qkv = w_i[:, :3 * ATT_DIM]
        w_rest = jnp.zeros((D_MODEL, PROJ_COLS), BF16)
        for col, piece in ((COL_Z, w_i[:, 3072:5120]), (COL_GATE, w_i[:, 9248:]), (COL_XBC, w_i[:, 5120:9216]),
                           (COL_DT, w_i[:, 9216:9248])):
            w_rest = lax.dynamic_update_slice(w_rest, piece, (0, col))
        gain_mix = norm_mix[i].reshape(1, D_MODEL)
        proj = in_proj(x, gain_mix, w_rest, tm=tm)

        q_gain = jnp.tile(q_norm[i], LANES // HEAD_DIM).reshape(1, LANES)
        k_gain = jnp.tile(k_norm[i], LANES // HEAD_DIM).reshape(1, LANES)
        qkv_w = (gain_mix, w_qkv)
        qb_p, k_p, kb_p, v_p, vb_p = qkv_proj(x, *qkv_w, cos_p, sin_p, q_gain, k_gain, row0=0, rows=tp,
                                              tm=math.gcd(tp, 512))
        qb_s, k_s, kb_s, v_s, vb_s = qkv_proj(x, *qkv_w, cos_s, sin_s, q_gain, k_gain, row0=tp, rows=ts,
                                              tm=math.gcd(math.gcd(tp, ts), 512))

        lam_rows = jnp.stack([lam_q1[i], lam_k1[i], lam_q2[i], lam_k2[i]])
        sub_gain = attn_subln[i].reshape(1, LANES)
        att_p = attn_prompt(qb_p, kb_p, vb_p, lam_rows, sub_gain, li, batch=bp, seq=lp,
                            tq=min(ATT_TQ, lp), tk=min(ATT_TK, lp), hps=ATT_HPS)
        att_s = attn_sample(qb_s, kb_s, vb_s, cache_kt, cache_v, lam_rows, sub_gain, li,
                            layer=i, batch=bs, rows=ls, tk=min(ATT_TKS, n_past))

        ssd_w = (conv_w[i], conv_b[i].reshape(1, CONV_DIM), _pad_lanes(dt_bias[i]), _pad_lanes(a_log[i]),
                 jnp.repeat(d_skip[i], SSM_HEAD_DIM).reshape(1, D_INNER), ssm_norm[i].reshape(1, D_INNER))
        y_p, ssm_p, conv_p = ssd(proj, conv0_p, ssm0_p, *ssd_w, batch=bp, seq=lp, row0=0, lc=min(CHUNK, lp))
        y_s, ssm_s, conv_s = ssd(proj, state_conv[i], state_ssm[i], *ssd_w, batch=bs, seq=ls, row0=tp,
                                 lc=min(CHUNK, ls))

        x = out_proj(x, att_p, att_s, y_p, y_s, proj, w_proj_attn[i].astype(BF16), w_proj_ssm[i].astype(BF16),
                     w_out[i].astype(BF16), tm=math.gcd(math.gcd(tp, ts), 512))

        h2, *tables = peer_keys(x, norm_ffn[i].reshape(1, D_MODEL), peer_wq[i].T.astype(BF16),
                                peer_k1[i].astype(BF16), peer_k2[i].astype(BF16), tt=tt)
        v_sub = peer_v[i].astype(BF16).reshape(N_EXPERTS // PEER_SUB, PEER_SUB, D_MODEL).transpose(0, 2, 1)
        x = peer_mix(x, h2, peer_u[i].astype(BF16), v_sub, *tables, tt=tt, eb=PEER_EB)

        outs["kp"].append(k_p.reshape(bp, lp, N_HEADS, 2, HEAD_DIM))
        outs["vp"].append(v_p.reshape(bp, lp, N_HEADS, 2 * HEAD_DIM))
        outs["sp"].append(ssm_p)
        outs["cp"].append(conv_p)
        outs["ks"].append(k_s.reshape(bs, ls, N_HEADS, 2, HEAD_DIM))
        outs["vs"].append(v_s.reshape(bs, ls, N_HEADS, 2 * HEAD_DIM))
        outs["ss"].append(ssm_s)
        outs["cs"].append(conv_s)

    stack = lambda name: jnp.stack(outs[name])
    return (x[:tp].reshape(bp, lp, D_MODEL), x[tp:].reshape(bs, ls, D_MODEL),
            stack("kp"), stack("vp"), stack("sp"), stack("cp"),
            stack("ks"), stack("vs"), stack("ss"), stack("cs"))
```

```python
import functools
import math

import jax
import jax.numpy as jnp
from jax import lax
from jax.experimental import pallas as pl
from jax.experimental.pallas import tpu as pltpu

F32 = jnp.float32
BF16 = jnp.bfloat16

D_MODEL = 1024
CHUNK = 64
N_HEADS = 8
HEAD_DIM = 64
ATT_DIM = N_HEADS * 2 * HEAD_DIM
ROPE_THETA = 10000.0
D_INNER = 2048
SSM_HEAD_DIM = 64
SSM_HEADS = D_INNER // SSM_HEAD_DIM
SSM_GROUPS = 8
HEADS_PER_GROUP = SSM_HEADS // SSM_GROUPS
D_STATE = 128
CONV_W = 4
CONV_DIM = D_INNER + 2 * SSM_GROUPS * D_STATE
PK_HEADS = 8
N_KEYS = 128
N_EXPERTS = N_KEYS * N_KEYS
PK_TOPK = 16
PK_HALF = 128
EPS = 1e-6

LANES = 128
SUBLANES = 8
V7X_VMEM_BYTES = 64 * 1024 * 1024
VMEM_LIMIT = V7X_VMEM_BYTES * 7 // 8

COL_Z = 0
COL_GATE = 2048
COL_XBC = 4096
COL_DT = 8192
PROJ_COLS = 8320
PROJ_TN = 1664
ATT_TQ = 512
ATT_TK = 512
ATT_TKS = 1024
ATT_HPS = 2
PEER_SUB = 512
PEER_EB = 8192

HIGHEST = lax.Precision.HIGHEST
NT_DIMS = (((1,), (1,)), ((), ()))


def _params(sem):
    return pltpu.CompilerParams(dimension_semantics=sem, vmem_limit_bytes=VMEM_LIMIT)


def _dot(a, b):
    return jnp.dot(a, b, preferred_element_type=F32)


def _dot_nt(a, b):
    return lax.dot_general(a, b, NT_DIMS, preferred_element_type=F32)


def _dot_exact(a, b):
    return jnp.dot(a, b, preferred_element_type=F32, precision=HIGHEST)


def _dot_split(a, b):
    head = a.astype(BF16)
    rest = (a - head.astype(F32)).astype(BF16)
    return _dot(head, b) + _dot(rest, b)


def _transpose_exact(x):
    c = x.shape[1]
    eye = (lax.broadcasted_iota(jnp.int32, (c, c), 0) == lax.broadcasted_iota(jnp.int32, (c, c), 1)).astype(F32)
    return lax.dot_general(eye, x, NT_DIMS, preferred_element_type=F32, precision=HIGHEST)


def _sigmoid(x):
    return 1.0 / (1.0 + jnp.exp(-x))


def _silu(x):
    return x * _sigmoid(x)


def _softplus(x):
    return jnp.maximum(x, 0.0) + jnp.log(1.0 + jnp.exp(-jnp.abs(x)))


def _in_proj_kernel(x_ref, g_ref, w_ref, o_ref, h_scr):
    @pl.when(pl.program_id(1) == 0)
    def _():
        x = x_ref[...]
        ms = jnp.mean(x * x, axis=-1, keepdims=True)
        h_scr[...] = (x * lax.rsqrt(ms + EPS) * g_ref[...]).astype(BF16)

    o_ref[...] = _dot(h_scr[...], w_ref[...])


def in_proj(x, gain, w, *, tm):
    t = x.shape[0]
    n = w.shape[1]
    return pl.pallas_call(
        _in_proj_kernel,
        out_shape=jax.ShapeDtypeStruct((t, n), F32),
        grid=(t // tm, n // PROJ_TN),
        in_specs=[pl.BlockSpec((tm, D_MODEL), lambda i, j: (i, 0)),
                  pl.BlockSpec((1, D_MODEL), lambda i, j: (0, 0)),
                  pl.BlockSpec((D_MODEL, PROJ_TN), lambda i, j: (0, j))],
        out_specs=pl.BlockSpec((tm, PROJ_TN), lambda i, j: (i, j)),
        scratch_shapes=[pltpu.VMEM((tm, D_MODEL), BF16)],
        compiler_params=_params(("parallel", "arbitrary")),
        name="in_proj",
    )(x, gain, w)


def _qkv_proj_kernel(x_ref, g_ref, w_ref, cos_ref, sin_ref, qg_ref, kg_ref,
                     qb_ref, ko_ref, kb_ref, vo_ref, vb_ref):
    x = x_ref[...]
    ms = jnp.mean(x * x, axis=-1, keepdims=True)
    h = (x * lax.rsqrt(ms + EPS) * g_ref[...]).astype(BF16)
    qkv = _dot(h, w_ref[...])
    cos = cos_ref[...]
    sin = sin_ref[...]
    row = lax.broadcasted_iota(jnp.int32, (LANES, LANES), 0) // HEAD_DIM
    col = lax.broadcasted_iota(jnp.int32, (LANES, LANES), 1) // HEAD_DIM
    group_mean = jnp.where(row == col, 1.0 / HEAD_DIM, 0.0).astype(BF16)
    lane = lax.broadcasted_iota(jnp.int32, (1, LANES), 1)
    low_half = (lane % HEAD_DIM) < (HEAD_DIM // 2)

    def norm_rope(y, gain):
        ms = _dot_split(y * y, group_mean)
        y = y * lax.rsqrt(ms + EPS) * gain
        partner = jnp.where(low_half, pltpu.roll(y, LANES - HEAD_DIM // 2, 1), pltpu.roll(y, HEAD_DIM // 2, 1))
        return y * cos + partner * sin

    for h_i in range(N_HEADS):
        sl = slice(h_i * LANES, (h_i + 1) * LANES)
        q = norm_rope(qkv[:, h_i * LANES:(h_i + 1) * LANES], qg_ref[...]) * (HEAD_DIM ** -0.5)
        qb_ref[:, sl] = q.astype(BF16)
        k = norm_rope(qkv[:, ATT_DIM + h_i * LANES:ATT_DIM + (h_i + 1) * LANES], kg_ref[...])
        ko_ref[:, sl] = k
        kb_ref[:, sl] = k.astype(BF16)
    v = qkv[:, 2 * ATT_DIM:]
    vo_ref[...] = v
    vb_ref[...] = v.astype(BF16)


def qkv_proj(x, gain, w_qkv, cos, sin, q_gain, k_gain, *, row0, rows, tm):
    t = rows
    tile0 = row0 // tm
    tab = pl.BlockSpec((tm, LANES), lambda i: (i, 0))
    vec = pl.BlockSpec((1, LANES), lambda i: (0, 0))
    out = pl.BlockSpec((tm, ATT_DIM), lambda i: (i, 0))
    return pl.pallas_call(
        _qkv_proj_kernel,
        out_shape=[jax.ShapeDtypeStruct((t, ATT_DIM), BF16),
                   jax.ShapeDtypeStruct((t, ATT_DIM), F32),
                   jax.ShapeDtypeStruct((t, ATT_DIM), BF16),
                   jax.ShapeDtypeStruct((t, ATT_DIM), F32),
                   jax.ShapeDtypeStruct((t, ATT_DIM), BF16)],
        grid=(t // tm,),
        in_specs=[pl.BlockSpec((tm, D_MODEL), lambda i: (tile0 + i, 0)),
                  pl.BlockSpec((1, D_MODEL), lambda i: (0, 0)),
                  pl.BlockSpec((D_MODEL, 3 * ATT_DIM), lambda i: (0, 0)),
                  tab, tab, vec, vec],
        out_specs=[out, out, out, out, out],
        compiler_params=_params(("parallel",)),
        name="qkv_proj",
    )(x, gain, w_qkv, cos, sin, q_gain, k_gain)


def _lambda_full(lam_ref, lam_init):
    a = jnp.sum(lam_ref[0:1, :] * lam_ref[1:2, :], axis=-1, keepdims=True)
    b = jnp.sum(lam_ref[2:3, :] * lam_ref[3:4, :], axis=-1, keepdims=True)
    return jnp.exp(a) - jnp.exp(b) + lam_init


def _split_components(q):
    lane = lax.broadcasted_iota(jnp.int32, (1, LANES), 1)
    zero = jnp.zeros_like(q)
    return jnp.concatenate([jnp.where(lane < HEAD_DIM, q, zero), jnp.where(lane >= HEAD_DIM, q, zero)], axis=0)


def _diff_combine(acc, l, rows, lam, gain, lam_init):
    o1 = acc[:rows] / l[:rows]
    o2 = acc[rows:] / l[rows:]
    o = o1 - lam * o2
    ms = jnp.mean(o * o, axis=-1, keepdims=True)
    return o * lax.rsqrt(ms + EPS) * gain * (1.0 - lam_init)


def _attn_prompt_kernel(lam_ref, g_ref, q_ref, k_ref, v_ref, o_ref,
                        vext_scr, s_scr, p_scr, m_scr, alpha_scr, acc_scr, *, tq, tk, hps, lam_init):
    i = pl.program_id(2)
    rows = 2 * tq
    heads = range(hps)

    @pl.when(i == 0)
    def _():
        for hh in heads:
            vext_scr[hh, :, :LANES] = v_ref[:, hh * LANES:(hh + 1) * LANES]
            vext_scr[hh, :, LANES:] = jnp.ones((vext_scr.shape[1], LANES), BF16)

    qq = [_split_components(q_ref[:, hh * LANES:(hh + 1) * LANES]) for hh in heads]
    diag = (i * tq) // tk
    n_blocks = diag + 1

    def key_block(t):
        return jnp.where(t == 0, diag, t - 1)

    def scores(hh, blk):
        return _dot_nt(qq[hh], k_ref[pl.ds(pl.multiple_of(blk * tk, tk), tk), hh * LANES:(hh + 1) * LANES])

    def softmax(hh, slot, bias=None):
        s = s_scr[hh, slot]
        if bias is not None:
            s = s + bias
        m_old = m_scr[hh]
        m_new = jnp.maximum(m_old, jnp.max(s, axis=-1, keepdims=True))
        alpha_scr[hh, slot] = jnp.exp(m_old - m_new)
        for c in range(tk // LANES):
            sl = slice(c * LANES, (c + 1) * LANES)
            p_scr[hh, slot, :, sl] = jnp.exp(s[:, sl] - m_new).astype(BF16)
        m_scr[hh] = m_new

    def accumulate(hh, slot, blk):
        pv = _dot(p_scr[hh, slot], vext_scr[hh, pl.ds(pl.multiple_of(blk * tk, tk), tk), :])
        alpha = alpha_scr[hh, slot]
        acc_scr[hh, :, :LANES] = alpha * acc_scr[hh, :, :LANES] + pv[:, :LANES]
        acc_scr[hh, :, LANES:] = alpha * acc_scr[hh, :, LANES:] + pv[:, LANES:]

    m_scr[...] = jnp.full(m_scr.shape, -jnp.inf, F32)
    acc_scr[...] = jnp.zeros(acc_scr.shape, F32)
    q_chunk = (i * tq + lax.broadcasted_iota(jnp.int32, (rows, 1), 0) % tq) // CHUNK
    k_chunk = (diag * tk + lax.broadcasted_iota(jnp.int32, (1, tk), 1)) // CHUNK
    for hh in heads:
        p_scr[hh, 1] = jnp.zeros(p_scr.shape[2:], BF16)
        alpha_scr[hh, 1] = jnp.ones(alpha_scr.shape[2:], F32)
        s_scr[hh, 0] = jnp.where(k_chunk <= q_chunk, scores(hh, diag), -jnp.inf)

    last = n_blocks - 1

    def body(pair, carry):
        t = 2 * pair
        for hh in heads:
            s_scr[hh, 1] = scores(hh, key_block(jnp.minimum(t + 1, last)))
            softmax(hh, 0)
            accumulate(hh, 1, key_block(jnp.maximum(t - 1, 0)))
        for hh in heads:
            s_scr[hh, 0] = scores(hh, key_block(jnp.minimum(t + 2, last)))
            softmax(hh, 1, jnp.where(t + 1 <= last, 0.0, -jnp.inf))
            accumulate(hh, 0, key_block(t))
        return carry

    n_pairs = (n_blocks + 1) // 2
    lax.fori_loop(0, n_pairs, body, 0)
    lam = _lambda_full(lam_ref, lam_init)
    for hh in heads:
        accumulate(hh, 1, key_block(jnp.minimum(2 * n_pairs - 1, last)))
        acc = acc_scr[hh]
        o_ref[:, hh * LANES:(hh + 1) * LANES] = _diff_combine(
            acc[:, :LANES], acc[:, LANES:], tq, lam, g_ref[...], lam_init).astype(o_ref.dtype)


def attn_prompt(qb, kb, vb, lam_rows, sub_gain, lam_init, *, batch, seq, tq, tk, hps):
    nq = seq // tq
    assert tk % tq == 0 and tq % CHUNK == 0, "one key block must hold all chunks of a query block"
    assert N_HEADS % hps == 0
    kern = functools.partial(_attn_prompt_kernel, tq=tq, tk=tk, hps=hps, lam_init=lam_init)
    width = hps * LANES
    return pl.pallas_call(
        kern,
        out_shape=jax.ShapeDtypeStruct((batch * seq, ATT_DIM), BF16),
        grid=(batch, N_HEADS // hps, nq),
        in_specs=[pl.BlockSpec((4, HEAD_DIM), lambda b, h, i: (0, 0)),
                  pl.BlockSpec((1, LANES), lambda b, h, i: (0, 0)),
                  pl.BlockSpec((tq, width), lambda b, h, i: (b * nq + i, h)),
                  pl.BlockSpec((seq, width), lambda b, h, i: (b, h)),
                  pl.BlockSpec((seq, width), lambda b, h, i: (b, h))],
        out_specs=pl.BlockSpec((tq, width), lambda b, h, i: (b * nq + i, h)),
        scratch_shapes=[pltpu.VMEM((hps, seq, 2 * LANES), BF16),
                        pltpu.VMEM((hps, 2, 2 * tq, tk), F32), pltpu.VMEM((hps, 2, 2 * tq, tk), BF16),
                        pltpu.VMEM((hps, 2 * tq, LANES), F32), pltpu.VMEM((hps, 2, 2 * tq, LANES), F32),
                        pltpu.VMEM((hps, 2 * tq, 2 * LANES), F32)],
        compiler_params=_params(("parallel", "parallel", "arbitrary")),
        name="attn_prompt",
    )(lam_rows, sub_gain, qb, kb, vb)


def _attn_sample_kernel(lam_ref, g_ref, q_ref, kn_ref, vn_ref, kt_ref, vc_ref, o_ref,
                        qq_scr, m_scr, l_scr, acc_scr, *, rows, tk, lam_init):
    j = pl.program_id(1)
    hr = 2 * rows

    @pl.when(j == 0)
    def _():
        for h in range(N_HEADS):
            qq_scr[h * hr:(h + 1) * hr, :] = _split_components(q_ref[:, h * LANES:(h + 1) * LANES])
        m_scr[...] = jnp.full(m_scr.shape, -jnp.inf, F32)
        l_scr[...] = jnp.zeros(l_scr.shape, F32)
        acc_scr[...] = jnp.zeros(acc_scr.shape, F32)

    def update(scores_of, values_of):
        s = jnp.concatenate([scores_of(h, qq_scr[h * hr:(h + 1) * hr, :]) for h in range(N_HEADS)], axis=0)
        m_old = m_scr[...]
        m_new = jnp.maximum(m_old, jnp.max(s, axis=-1, keepdims=True))
        alpha = jnp.exp(m_old - m_new)
        p = jnp.exp(s - m_new)
        l_scr[...] = alpha * l_scr[...] + jnp.sum(p, axis=-1, keepdims=True)
        m_scr[...] = m_new
        p = p.astype(BF16)
        for h in range(N_HEADS):
            hs = slice(h * hr, (h + 1) * hr)
            acc_scr[hs, :] = alpha[hs] * acc_scr[hs, :] + _dot(p[hs], values_of(h))

    update(lambda h, qq: _dot(qq, kt_ref[h * LANES:(h + 1) * LANES, :].astype(BF16)),
           lambda h: vc_ref[pl.ds(h, tk, stride=N_HEADS), :].astype(BF16))

    @pl.when(j == pl.num_programs(1) - 1)
    def _():
        update(lambda h, qq: _dot_nt(qq, kn_ref[:, h * LANES:(h + 1) * LANES]),
               lambda h: vn_ref[:, h * LANES:(h + 1) * LANES])
        lam = _lambda_full(lam_ref, lam_init)
        for h in range(N_HEADS):
            hs = slice(h * hr, (h + 1) * hr)
            o_ref[:, h * LANES:(h + 1) * LANES] = _diff_combine(
                acc_scr[hs, :], l_scr[hs, :], rows, lam, g_ref[...], lam_init).astype(o_ref.dtype)


def attn_sample(qb, kb, vb, cache_kt, cache_v, lam_rows, sub_gain, lam_init, *, layer, batch, rows, tk):
    past = cache_kt.shape[3]
    new_blk = pl.BlockSpec((rows, ATT_DIM), lambda b, j: (b, 0))
    kern = functools.partial(_attn_sample_kernel, rows=rows, tk=tk, lam_init=lam_init)
    return pl.pallas_call(
        kern,
        out_shape=jax.ShapeDtypeStruct((batch * rows, ATT_DIM), BF16),
        grid=(batch, past // tk),
        in_specs=[pl.BlockSpec((4, HEAD_DIM), lambda b, j: (0, 0)),
                  pl.BlockSpec((1, LANES), lambda b, j: (0, 0)),
                  new_blk, new_blk, new_blk,
                  pl.BlockSpec((None, None, ATT_DIM, tk), lambda b, j: (layer, b, 0, j)),
                  pl.BlockSpec((None, None, tk * N_HEADS, 2 * HEAD_DIM), lambda b, j: (layer, b, j, 0))],
        out_specs=pl.BlockSpec((rows, ATT_DIM), lambda b, j: (b, 0)),
        scratch_shapes=[pltpu.VMEM((N_HEADS * 2 * rows, LANES), BF16),
                        pltpu.VMEM((N_HEADS * 2 * rows, 1), F32), pltpu.VMEM((N_HEADS * 2 * rows, 1), F32),
                        pltpu.VMEM((N_HEADS * 2 * rows, LANES), F32)],
        compiler_params=_params(("parallel", "arbitrary")),
        name="attn_sample",
    )(lam_rows, sub_gain, qb, kb, vb, cache_kt, cache_v)


def _ssd_kernel(xbc_ref, z_ref, dt_ref, conv0_ref, ssm0_ref, cw_ref, cb_ref, dtb_ref, alog_ref, dskip_ref, ng_ref,
                y_ref, ssm_ref, conv_ref, xpad_scr, *, lc):
    c = pl.program_id(1)
    pad0 = SUBLANES - (CONV_W - 1)

    @pl.when(c == 0)
    def _():
        xpad_scr[pad0:SUBLANES, :] = conv0_ref[...]
        ssm_ref[...] = ssm0_ref[...]

    xpad_scr[SUBLANES:, :] = xbc_ref[...]
    conv = cb_ref[...] + sum(xpad_scr[pad0 + j:pad0 + j + lc, :] * cw_ref[j:j + 1, :] for j in range(CONV_W))
    conv = _silu(conv)
    tail = xpad_scr[lc + pad0:lc + SUBLANES, :]
    xpad_scr[pad0:SUBLANES, :] = tail
    conv_ref[...] = tail

    dt = _softplus(dt_ref[...] + dtb_ref[...])
    da = dt * (-jnp.exp(alog_ref[...]))
    row = lax.broadcasted_iota(jnp.int32, (lc, lc), 0)
    col = lax.broadcasted_iota(jnp.int32, (lc, lc), 1)
    causal = row >= col
    a_cum = _dot_exact(causal.astype(F32), da)
    dt_t = _transpose_exact(dt)
    a_cum_t = _transpose_exact(a_cum)
    eye = (lax.broadcasted_iota(jnp.int32, (LANES, LANES), 0)
           == lax.broadcasted_iota(jnp.int32, (LANES, LANES), 1)).astype(BF16)

    for g in range(SSM_GROUPS):
        b_g = conv[:, D_INNER + g * D_STATE:D_INNER + (g + 1) * D_STATE].astype(BF16)
        c_off = D_INNER + SSM_GROUPS * D_STATE
        c_g = conv[:, c_off + g * D_STATE:c_off + (g + 1) * D_STATE].astype(BF16)
        cb = _dot_nt(c_g, b_g)
        ys = []
        for pair in range(HEADS_PER_GROUP // 2):
            lane0 = (g * HEADS_PER_GROUP + 2 * pair) * SSM_HEAD_DIM
            x_pair = conv[:, lane0:lane0 + LANES].astype(BF16)
            x_pair_t = _dot_nt(eye, x_pair)
            for sub in range(2):
                h = g * HEADS_PER_GROUP + 2 * pair + sub
                x_h = x_pair[:, sub * SSM_HEAD_DIM:(sub + 1) * SSM_HEAD_DIM]
                x_h_t = x_pair_t[sub * SSM_HEAD_DIM:(sub + 1) * SSM_HEAD_DIM, :]
                a_col = a_cum[:, h:h + 1]
                a_row = a_cum_t[h:h + 1, :]
                dt_row = dt_t[h:h + 1, :]
                a_last = a_cum_t[h:h + 1, lc - 1:lc]
                decay = jnp.exp(jnp.where(causal, a_col - a_row, -jnp.inf))
                w_in = (cb * decay * dt_row).astype(BF16)
                state = ssm_ref[h]
                y_h = _dot(w_in, x_h) + jnp.exp(a_col) * _dot_nt(c_g, state.astype(BF16))
                ys.append(y_h)
                w_state = (x_h_t * (dt_row * jnp.exp(a_last - a_row))).astype(BF16)
                ssm_ref[h] = jnp.exp(a_last) * state + _dot(w_state, b_g)
        lanes = slice(g * HEADS_PER_GROUP * SSM_HEAD_DIM, (g + 1) * HEADS_PER_GROUP * SSM_HEAD_DIM)
        y_g = jnp.concatenate(ys, axis=-1) + dskip_ref[:, lanes] * conv[:, lanes]
        y_g = y_g * _silu(z_ref[:, lanes])
        ms = jnp.mean(y_g * y_g, axis=-1, keepdims=True)
        y_ref[:, lanes] = (y_g * lax.rsqrt(ms + EPS) * ng_ref[:, lanes]).astype(y_ref.dtype)


def ssd(proj, conv0, ssm0, conv_w, conv_b, dt_bias, a_log, d_skip, norm_gain, *, batch, seq, row0, lc):
    nc = seq // lc
    tok = lambda width, col: pl.BlockSpec((lc, width), lambda b, c: (row0 // lc + b * nc + c, col // width))
    const = lambda r, w: pl.BlockSpec((r, w), lambda b, c: (0, 0))
    kern = functools.partial(_ssd_kernel, lc=lc)
    return pl.pallas_call(
        kern,
        out_shape=[jax.ShapeDtypeStruct((batch * seq, D_INNER), BF16),
                   jax.ShapeDtypeStruct((batch, SSM_HEADS, SSM_HEAD_DIM, D_STATE), F32),
                   jax.ShapeDtypeStruct((batch, CONV_W - 1, CONV_DIM), F32)],
        grid=(batch, nc),
        in_specs=[tok(CONV_DIM, COL_XBC), tok(D_INNER, COL_Z), tok(LANES, COL_DT),
                  pl.BlockSpec((None, CONV_W - 1, CONV_DIM), lambda b, c: (b, 0, 0)),
                  pl.BlockSpec((None, SSM_HEADS, SSM_HEAD_DIM, D_STATE), lambda b, c: (b, 0, 0, 0)),
                  const(CONV_W, CONV_DIM), const(1, CONV_DIM), const(1, LANES), const(1, LANES),
                  const(1, D_INNER), const(1, D_INNER)],
        out_specs=[pl.BlockSpec((lc, D_INNER), lambda b, c: (b * nc + c, 0)),
                   pl.BlockSpec((None, SSM_HEADS, SSM_HEAD_DIM, D_STATE), lambda b, c: (b, 0, 0, 0)),
                   pl.BlockSpec((None, CONV_W - 1, CONV_DIM), lambda b, c: (b, 0, 0))],
        scratch_shapes=[pltpu.VMEM((SUBLANES + lc, CONV_DIM), F32)],
        compiler_params=_params(("parallel", "arbitrary")),
        name="ssd",
    )(proj, proj, proj, conv0, ssm0, conv_w, conv_b, dt_bias, a_log, d_skip, norm_gain)


def _out_proj_kernel(x_ref, attp_ref, atts_ref, yp_ref, ys_ref, ga_ref, gm_ref, wa_ref, wm_ref, wo_ref, o_ref, *,
                     prompt_tiles):
    is_prompt = pl.program_id(0) < prompt_tiles
    att = jnp.where(is_prompt, attp_ref[...], atts_ref[...])
    y = jnp.where(is_prompt, yp_ref[...], ys_ref[...])
    merged = (_sigmoid(ga_ref[...]) * _dot(att, wa_ref[...]) + _sigmoid(gm_ref[...]) * _dot(y, wm_ref[...]))
    o_ref[...] = x_ref[...] + _dot(merged.astype(BF16), wo_ref[...])


def out_proj(x, att_p, att_s, y_p, y_s, proj, wa, wm, wo, *, tm):
    t = x.shape[0]
    n_p, n_s = att_p.shape[0] // tm, att_s.shape[0] // tm
    assert n_p * tm == att_p.shape[0] and n_s * tm == att_s.shape[0] and n_p + n_s == t // tm
    gate = lambda c: pl.BlockSpec((tm, D_MODEL), lambda i, c=c: (i, c // D_MODEL))
    const = lambda r: pl.BlockSpec((r, D_MODEL), lambda i: (0, 0))
    prompt = lambda w: pl.BlockSpec((tm, w), lambda i: (jnp.minimum(i, n_p - 1), 0))
    sample = lambda w: pl.BlockSpec((tm, w), lambda i: (jnp.maximum(i - n_p, 0), 0))
    return pl.pallas_call(
        functools.partial(_out_proj_kernel, prompt_tiles=n_p),
        out_shape=jax.ShapeDtypeStruct((t, D_MODEL), F32),
        grid=(t // tm,),
        in_specs=[pl.BlockSpec((tm, D_MODEL), lambda i: (i, 0)),
                  prompt(ATT_DIM), sample(ATT_DIM), prompt(D_INNER), sample(D_INNER),
                  gate(COL_GATE), gate(COL_GATE + D_MODEL),
                  const(ATT_DIM), const(D_INNER), const(D_MODEL)],
        out_specs=pl.BlockSpec((tm, D_MODEL), lambda i: (i, 0)),
        compiler_params=_params(("parallel",)),
        name="out_proj",
    )(x, att_p, att_s, y_p, y_s, proj, proj, wa, wm, wo)


def _top_values(s, k):
    vals = []
    for _ in range(k):
        m = jnp.max(s, axis=0, keepdims=True)
        vals.append(m)
        s = jnp.where(s == m, -jnp.inf, s)
    return vals


def _top_ranked(s, k):
    vals = []
    rank = jnp.full(s.shape, float(k), F32)
    for i in range(k):
        m = jnp.max(s, axis=0, keepdims=True)
        hit = s == m
        vals.append(m)
        rank = jnp.where(hit, float(i), rank)
        s = jnp.where(hit, -jnp.inf, s)
    return vals, rank


def _peer_keys_kernel(x_ref, g_ref, wqt_ref, k1_ref, k2_ref, ht_ref, e1_ref, cnt_ref, e2_ref, r2_ref):
    x = x_ref[...]
    ms = jnp.mean(x * x, axis=-1, keepdims=True)
    h32 = x * lax.rsqrt(ms + EPS) * g_ref[...]
    h_t = h32.T.astype(BF16)
    ht_ref[...] = h_t
    q_t = _dot(wqt_ref[...], h_t)
    for hd in range(PK_HEADS):
        r0 = hd * 2 * PK_HALF
        s1 = _dot(k1_ref[hd], q_t[r0:r0 + PK_HALF, :].astype(BF16))
        s2 = _dot(k2_ref[hd], q_t[r0 + PK_HALF:r0 + 2 * PK_HALF, :].astype(BF16))
        v1 = _top_values(s1, PK_TOPK)
        v2, r2 = _top_ranked(s2, PK_TOPK)
        widths = [PK_TOPK // (a + 1) for a in range(PK_TOPK)]
        pad_rows = -sum(widths) % SUBLANES
        cand = jnp.concatenate([v1[a] + jnp.concatenate(v2[:widths[a]], axis=0) for a in range(PK_TOPK)]
                               + [jnp.full((pad_rows, s1.shape[1]), -jnp.inf, F32)], axis=0)
        tau = _top_values(cand, PK_TOPK)[-1]
        top = v1[0] + v2[0]
        chosen = cand >= tau
        z = jnp.sum(jnp.where(chosen, jnp.exp(cand - top), 0.0), axis=0, keepdims=True)
        cnt = jnp.zeros(s1.shape, F32)
        row0 = 0
        for a in range(PK_TOPK):
            n_a = jnp.sum(chosen[row0:row0 + widths[a]].astype(F32), axis=0, keepdims=True)
            cnt = jnp.where(s1 == v1[a], n_a, cnt)
            row0 += widths[a]
        e1_ref[hd] = jnp.exp(s1 - v1[0]) / z
        cnt_ref[hd] = cnt
        e2_ref[hd] = jnp.exp(s2 - v2[0])
        r2_ref[hd] = r2


def peer_keys(x, gain, wq_t, k1, k2, *, tt):
    t = x.shape[0]
    per_head = pl.BlockSpec((PK_HEADS, N_KEYS, tt), lambda i: (0, 0, i))
    table = jax.ShapeDtypeStruct((PK_HEADS, N_KEYS, t), F32)
    return pl.pallas_call(
        _peer_keys_kernel,
        out_shape=[jax.ShapeDtypeStruct((D_MODEL, t), BF16), table, table, table, table],
        grid=(t // tt,),
        in_specs=[pl.BlockSpec((tt, D_MODEL), lambda i: (i, 0)),
                  pl.BlockSpec((1, D_MODEL), lambda i: (0, 0)),
                  pl.BlockSpec((PK_HEADS * 2 * PK_HALF, D_MODEL), lambda i: (0, 0)),
                  pl.BlockSpec((PK_HEADS, N_KEYS, PK_HALF), lambda i: (0, 0, 0)),
                  pl.BlockSpec((PK_HEADS, N_KEYS, PK_HALF), lambda i: (0, 0, 0))],
        out_specs=[pl.BlockSpec((D_MODEL, tt), lambda i: (0, i)), per_head, per_head, per_head, per_head],
        compiler_params=_params(("parallel",)),
        name="peer_keys",
    )(x, gain, wq_t, k1, k2)


def _gelu(x):
    return 0.5 * x * (1.0 + lax.erf(x * (2.0 ** -0.5)))


def _peer_mix_kernel(*refs, block, eb, first, last):
    refs = list(refs)
    h_ref, u_ref, vt_ref, e1_ref, cnt_ref, e2_ref, r2_ref = refs[:7]
    del refs[:7]
    accin_ref = None if first else refs.pop(0)
    x_ref = refs.pop(0) if last else None
    acc_ref = refs.pop(0)
    o_ref = refs.pop(0) if last else None
    raw_scr, coef_scr, e2_scr, r2_scr = refs
    tt = acc_ref.shape[1]
    n_sub = eb // PEER_SUB
    rows_per_sub = PEER_SUB // N_KEYS
    pack = 2 * SUBLANES

    acc_ref[...] = jnp.zeros(acc_ref.shape, F32) if first else accin_ref[...]
    for hd in range(PK_HEADS):
        e2_scr[hd] = e2_ref[hd].astype(BF16)
        r2_scr[hd] = r2_ref[hd].astype(BF16)

    def pre_activations(j):
        start = pl.multiple_of(j * PEER_SUB, PEER_SUB)
        return _dot(u_ref[pl.ds(start, PEER_SUB), :], h_ref[...])

    def mix(slot, j):
        zero = jnp.zeros((pack, LANES), BF16)
        for r in range(rows_per_sub):
            i1 = (block * n_sub + j) * rows_per_sub + r
            cnt_rows = [cnt_ref[hd, pl.ds(i1, 1), :] for hd in range(PK_HEADS)]
            e1_rows = [e1_ref[hd, pl.ds(i1, 1), :] for hd in range(PK_HEADS)]
            for lt in range(tt // LANES):
                lanes = slice(lt * LANES, (lt + 1) * LANES)
                cnt_b = [jnp.broadcast_to(row[:, lanes], (pack, LANES)).astype(BF16) for row in cnt_rows]
                e1_b = [jnp.broadcast_to(row[:, lanes], (pack, LANES)).astype(BF16) for row in e1_rows]
                for c in range(N_KEYS // pack):
                    keys = slice(c * pack, (c + 1) * pack)
                    gate = zero
                    for hd in range(PK_HEADS):
                        keep = r2_scr[hd, keys, lanes] < cnt_b[hd]
                        gate = gate + jnp.where(keep, e2_scr[hd, keys, lanes] * e1_b[hd], zero)
                    rows = slice(r * N_KEYS + c * pack, r * N_KEYS + (c + 1) * pack)
                    coef_scr[slot, rows, lanes] = gate * _gelu(raw_scr[slot, rows, lanes]).astype(BF16)

    def accumulate(slot, j):
        acc_ref[...] += _dot(vt_ref[j], coef_scr[slot])

    raw_scr[0] = pre_activations(0)
    coef_scr[1] = jnp.zeros(coef_scr.shape[1:], BF16)

    def body(pair, carry):
        j = 2 * pair
        raw_scr[1] = pre_activations(j + 1)
        mix(0, j)
        accumulate(1, jnp.maximum(j - 1, 0))
        raw_scr[0] = pre_activations(jnp.minimum(j + 2, n_sub - 1))
        mix(1, j + 1)
        accumulate(0, j)
        return carry

    lax.fori_loop(0, n_sub // 2, body, 0)
    accumulate(1, n_sub - 1)
    if last:
        o_ref[...] = x_ref[...] + acc_ref[...].T


def peer_mix(x, h_t, u, v_sub, e1, cnt, e2, r2, *, tt, eb):
    t = x.shape[0]
    n_blocks = N_EXPERTS // eb
    assert (eb // PEER_SUB) % 2 == 0
    per_head = pl.BlockSpec((PK_HEADS, N_KEYS, tt), lambda i: (0, 0, i))
    acc_blk = pl.BlockSpec((D_MODEL, tt), lambda i: (0, i))
    tok_blk = pl.BlockSpec((tt, D_MODEL), lambda i: (i, 0))
    acc_shape = jax.ShapeDtypeStruct((D_MODEL, t), F32)
    acc = None
    for block in range(n_blocks):
        first, last = block == 0, block == n_blocks - 1
        args = [h_t, u, v_sub, e1, cnt, e2, r2] + ([] if first else [acc]) + ([x] if last else [])
        in_specs = ([acc_blk,
                     pl.BlockSpec((eb, D_MODEL), lambda i, block=block: (block, 0), pipeline_mode=pl.Buffered(1)),
                     pl.BlockSpec((eb // PEER_SUB, D_MODEL, PEER_SUB), lambda i, block=block: (block, 0, 0),
                                  pipeline_mode=pl.Buffered(1)),
                     per_head, per_head, per_head, per_head]
                    + ([] if first else [acc_blk]) + ([tok_blk] if last else []))
        result = pl.pallas_call(
            functools.partial(_peer_mix_kernel, block=block, eb=eb, first=first, last=last),
            out_shape=[acc_shape] + ([jax.ShapeDtypeStruct((t, D_MODEL), F32)] if last else []),
            grid=(t // tt,),
            in_specs=in_specs,
            out_specs=[acc_blk] + ([tok_blk] if last else []),
            scratch_shapes=[pltpu.VMEM((2, PEER_SUB, tt), F32), pltpu.VMEM((2, PEER_SUB, tt), BF16),
                            pltpu.VMEM((PK_HEADS, N_KEYS, tt), BF16), pltpu.VMEM((PK_HEADS, N_KEYS, tt), BF16)],
            input_output_aliases={} if first else {7: 0},
            compiler_params=_params(("parallel",)),
            name="peer_mix",
        )(*args)
        acc = result[0]
    return result[1]


def _rope_tables(pos):
    half = HEAD_DIM // 2
    inv_freq = ROPE_THETA ** (-jnp.arange(half, dtype=F32) / half)
    ang = pos[:, None] * inv_freq[None, :]
    cos, sin = jnp.cos(ang), jnp.sin(ang)
    reps = LANES // HEAD_DIM
    return jnp.tile(jnp.concatenate([cos, cos], -1), (1, reps)), jnp.tile(jnp.concatenate([-sin, sin], -1), (1, reps))


def _lambda_init(layer):
    return 0.8 - 0.6 * math.exp(-0.3 * layer)


def _pad_lanes(v, width=LANES):
    return jnp.pad(v, (0, width - v.shape[0])).reshape(1, width)


def kernel(x_prompt, x_sample, cache_k, cache_v, state_ssm, state_conv, norm_mix, w_in, q_norm, k_norm, lam_q1, lam_k1, lam_q2, lam_k2, attn_subln, conv_w, conv_b, dt_bias, a_log, d_skip, ssm_norm, w_proj_attn, w_proj_ssm, w_out, norm_ffn, peer_wq, peer_k1, peer_k2, peer_u, peer_v):
    bp, lp, _ = x_prompt.shape
    bs, ls, _ = x_sample.shape
    depth = w_in.shape[0]
    n_past = cache_k.shape[2]
    tp, ts = bp * lp, bs * ls
    x = jnp.concatenate([x_prompt.reshape(tp, D_MODEL), x_sample.reshape(ts, D_MODEL)], axis=0)
    t_all = tp + ts
    tm = math.gcd(t_all, 1024)
    tt = math.gcd(t_all, 256)

    cos_p, sin_p = _rope_tables(jnp.tile(jnp.arange(lp, dtype=F32), bp))
    cos_s, sin_s = _rope_tables(jnp.tile(n_past + jnp.arange(ls, dtype=F32), bs))
    cache_kt = jnp.transpose(cache_k, (0, 1, 3, 4, 5, 2)).reshape(depth, bs, ATT_DIM, n_past)
    cache_v = cache_v.reshape(depth, bs, n_past * N_HEADS, 2 * HEAD_DIM)
    conv0_p = jnp.zeros((bp, CONV_W - 1, CONV_DIM), F32)
    ssm0_p = jnp.zeros((bp, SSM_HEADS, SSM_HEAD_DIM, D_STATE), F32)

    outs = {name: [] for name in ("kp", "vp", "sp", "cp", "ks", "vs", "ss", "cs")}
    for i in range(depth):
        li = _lambda_init(i)
        w_i = w_in[i].astype(BF16)
        w_qkv = w_i[:, :3 * ATT_DIM]
        w_rest = jnp.zeros((D_MODEL, PROJ_COLS), BF16)
        for col, piece in ((COL_Z, w_i[:, 3072:5120]), (COL_GATE, w_i[:, 9248:]), (COL_XBC, w_i[:, 5120:9216]),
                           (COL_DT, w_i[:, 9216:9248])):
            w_rest = lax.dynamic_update_slice(w_rest, piece, (0, col))
        gain_mix = norm_mix[i].reshape(1, D_MODEL)
        proj = in_proj(x, gain_mix, w_rest, tm=tm)

        q_gain = jnp.tile(q_norm[i], LANES // HEAD_DIM).reshape(1, LANES)
        k_gain = jnp.tile(k_norm[i], LANES // HEAD_DIM).reshape(1, LANES)
        qkv_w = (gain_mix, w_qkv)
        qb_p, k_p, kb_p, v_p, vb_p = qkv_proj(x, *qkv_w, cos_p, sin_p, q_gain, k_gain, row0=0, rows=tp,
                                              tm=math.gcd(tp, 512))
        qb_s, k_s, kb_s, v_s, vb_s = qkv_proj(x, *qkv_w, cos_s, sin_s, q_gain, k_gain, row0=tp, rows=ts,
                                              tm=math.gcd(math.gcd(tp, ts), 512))

        lam_rows = jnp.stack([lam_q1[i], lam_k1[i], lam_q2[i], lam_k2[i]])
        sub_gain = attn_subln[i].reshape(1, LANES)
        att_p = attn_prompt(qb_p, kb_p, vb_p, lam_rows, sub_gain, li, batch=bp, seq=lp,
                            tq=min(ATT_TQ, lp), tk=min(ATT_TK, lp), hps=ATT_HPS)
        att_s = attn_sample(qb_s, kb_s, vb_s, cache_kt, cache_v, lam_rows, sub_gain, li,
                            layer=i, batch=bs, rows=ls, tk=min(ATT_TKS, n_past))

        ssd_w = (conv_w[i], conv_b[i].reshape(1, CONV_DIM), _pad_lanes(dt_bias[i]), _pad_lanes(a_log[i]),
                 jnp.repeat(d_skip[i], SSM_HEAD_DIM).reshape(1, D_INNER), ssm_norm[i].reshape(1, D_INNER))
        y_p, ssm_p, conv_p = ssd(proj, conv0_p, ssm0_p, *ssd_w, batch=bp, seq=lp, row0=0, lc=min(CHUNK, lp))
        y_s, ssm_s, conv_s = ssd(proj, state_conv[i], state_ssm[i], *ssd_w, batch=bs, seq=ls, row0=tp,
                                 lc=min(CHUNK, ls))

        x = out_proj(x, att_p, att_s, y_p, y_s, proj, w_proj_attn[i].astype(BF16), w_proj_ssm[i].astype(BF16),
                     w_out[i].astype(BF16), tm=math.gcd(math.gcd(tp, ts), 512))

        h2, *tables = peer_keys(x, norm_ffn[i].reshape(1, D_MODEL), peer_wq[i].T.astype(BF16),
                                peer_k1[i].astype(BF16), peer_k2[i].astype(BF16), tt=tt)
        v_sub = peer_v[i].astype(BF16).reshape(N_EXPERTS // PEER_SUB, PEER_SUB, D_MODEL).transpose(0, 2, 1)
        x = peer_mix(x, h2, peer_u[i].astype(BF16), v_sub, *tables, tt=tt, eb=PEER_EB)

        outs["kp"].append(k_p.reshape(bp, lp, N_HEADS, 2, HEAD_DIM))
        outs["vp"].append(v_p.reshape(bp, lp, N_HEADS, 2 * HEAD_DIM))
        outs["sp"].append(ssm_p)
        outs["cp"].append(conv_p)
        outs["ks"].append(k_s.reshape(bs, ls, N_HEADS, 2, HEAD_DIM))
        outs["vs"].append(v_s.reshape(bs, ls, N_HEADS, 2 * HEAD_DIM))
        outs["ss"].append(ssm_s)
        outs["cs"].append(conv_s)

    stack = lambda name: jnp.stack(outs[name])
    return (x[:tp].reshape(bp, lp, D_MODEL), x[tp:].reshape(bs, ls, D_MODEL),
            stack("kp"), stack("vp"), stack("sp"), stack("cp"),
            stack("ks"), stack("vs"), stack("ss"), stack("cs"))
```

```python
import functools
import math

import jax
import jax.numpy as jnp
from jax import lax
from jax.experimental import pallas as pl
from jax.experimental.pallas import tpu as pltpu

F32 = jnp.float32
BF16 = jnp.bfloat16

D_MODEL = 1024
CHUNK = 64
N_HEADS = 8
HEAD_DIM = 64
ATT_DIM = N_HEADS * 2 * HEAD_DIM
ROPE_THETA = 10000.0
D_INNER = 2048
SSM_HEAD_DIM = 64
SSM_HEADS = D_INNER // SSM_HEAD_DIM
SSM_GROUPS = 8
HEADS_PER_GROUP = SSM_HEADS // SSM_GROUPS
D_STATE = 128
CONV_W = 4
CONV_DIM = D_INNER + 2 * SSM_GROUPS * D_STATE
PK_HEADS = 8
N_KEYS = 128
N_EXPERTS = N_KEYS * N_KEYS
PK_TOPK = 16
PK_HALF = 128
EPS = 1e-6

LANES = 128
SUBLANES = 8
V7X_VMEM_BYTES = 64 * 1024 * 1024
VMEM_LIMIT = V7X_VMEM_BYTES * 7 // 8

COL_Z = 0
COL_GATE = 2048
COL_XBC = 4096
COL_DT = 8192
PROJ_COLS = 8320
PROJ_TN = 1664
ATT_TQ = 512
ATT_TK = 512
ATT_TKS = 1024
ATT_HPS = 2
SSD_CHUNK = 128
PEER_SUB = 512
PEER_EB = 8192

HIGHEST = lax.Precision.HIGHEST
NT_DIMS = (((1,), (1,)), ((), ()))


def _params(sem):
    return pltpu.CompilerParams(dimension_semantics=sem, vmem_limit_bytes=VMEM_LIMIT)


def _dot(a, b):
    return jnp.dot(a, b, preferred_element_type=F32)


def _dot_nt(a, b):
    return lax.dot_general(a, b, NT_DIMS, preferred_element_type=F32)


def _dot_exact(a, b):
    return jnp.dot(a, b, preferred_element_type=F32, precision=HIGHEST)


def _dot_split(a, b):
    head = a.astype(BF16)
    rest = (a - head.astype(F32)).astype(BF16)
    return _dot(head, b) + _dot(rest, b)


def _transpose_exact(x):
    c = x.shape[1]
    eye = (lax.broadcasted_iota(jnp.int32, (c, c), 0) == lax.broadcasted_iota(jnp.int32, (c, c), 1)).astype(F32)
    return lax.dot_general(eye, x, NT_DIMS, preferred_element_type=F32, precision=HIGHEST)


def _sigmoid(x):
    return 1.0 / (1.0 + jnp.exp(-x))


def _silu(x):
    return x * _sigmoid(x)


def _softplus(x):
    return jnp.maximum(x, 0.0) + jnp.log(1.0 + jnp.exp(-jnp.abs(x)))


def _in_proj_kernel(x_ref, g_ref, w_ref, o_ref, h_scr):
    @pl.when(pl.program_id(1) == 0)
    def _():
        x = x_ref[...]
        ms = jnp.mean(x * x, axis=-1, keepdims=True)
        h_scr[...] = (x * lax.rsqrt(ms + EPS) * g_ref[...]).astype(BF16)

    o_ref[...] = _dot(h_scr[...], w_ref[...])


def in_proj(x, gain, w, *, tm):
    t = x.shape[0]
    n = w.shape[1]
    return pl.pallas_call(
        _in_proj_kernel,
        out_shape=jax.ShapeDtypeStruct((t, n), F32),
        grid=(t // tm, n // PROJ_TN),
        in_specs=[pl.BlockSpec((tm, D_MODEL), lambda i, j: (i, 0)),
                  pl.BlockSpec((1, D_MODEL), lambda i, j: (0, 0)),
                  pl.BlockSpec((D_MODEL, PROJ_TN), lambda i, j: (0, j))],
        out_specs=pl.BlockSpec((tm, PROJ_TN), lambda i, j: (i, j)),
        scratch_shapes=[pltpu.VMEM((tm, D_MODEL), BF16)],
        compiler_params=_params(("parallel", "arbitrary")),
        name="in_proj",
    )(x, gain, w)


def _qkv_proj_kernel(x_ref, g_ref, w_ref, cos_ref, sin_ref, qg_ref, kg_ref,
                     qb_ref, ko_ref, kb_ref, vo_ref, vb_ref):
    x = x_ref[...]
    ms = jnp.mean(x * x, axis=-1, keepdims=True)
    h = (x * lax.rsqrt(ms + EPS) * g_ref[...]).astype(BF16)
    qkv = _dot(h, w_ref[...])
    cos = cos_ref[...]
    sin = sin_ref[...]
    row = lax.broadcasted_iota(jnp.int32, (LANES, LANES), 0) // HEAD_DIM
    col = lax.broadcasted_iota(jnp.int32, (LANES, LANES), 1) // HEAD_DIM
    group_mean = jnp.where(row == col, 1.0 / HEAD_DIM, 0.0).astype(BF16)
    lane = lax.broadcasted_iota(jnp.int32, (1, LANES), 1)
    low_half = (lane % HEAD_DIM) < (HEAD_DIM // 2)

    def norm_rope(y, gain):
        ms = _dot_split(y * y, group_mean)
        y = y * lax.rsqrt(ms + EPS) * gain
        partner = jnp.where(low_half, pltpu.roll(y, LANES - HEAD_DIM // 2, 1), pltpu.roll(y, HEAD_DIM // 2, 1))
        return y * cos + partner * sin

    for h_i in range(N_HEADS):
        sl = slice(h_i * LANES, (h_i + 1) * LANES)
        q = norm_rope(qkv[:, h_i * LANES:(h_i + 1) * LANES], qg_ref[...]) * (HEAD_DIM ** -0.5)
        qb_ref[:, sl] = q.astype(BF16)
        k = norm_rope(qkv[:, ATT_DIM + h_i * LANES:ATT_DIM + (h_i + 1) * LANES], kg_ref[...])
        ko_ref[:, sl] = k
        kb_ref[:, sl] = k.astype(BF16)
    v = qkv[:, 2 * ATT_DIM:]
    vo_ref[...] = v
    vb_ref[...] = v.astype(BF16)


def qkv_proj(x, gain, w_qkv, cos, sin, q_gain, k_gain, *, row0, rows, tm):
    t = rows
    tile0 = row0 // tm
    tab = pl.BlockSpec((tm, LANES), lambda i: (i, 0))
    vec = pl.BlockSpec((1, LANES), lambda i: (0, 0))
    out = pl.BlockSpec((tm, ATT_DIM), lambda i: (i, 0))
    return pl.pallas_call(
        _qkv_proj_kernel,
        out_shape=[jax.ShapeDtypeStruct((t, ATT_DIM), BF16),
                   jax.ShapeDtypeStruct((t, ATT_DIM), F32),
                   jax.ShapeDtypeStruct((t, ATT_DIM), BF16),
                   jax.ShapeDtypeStruct((t, ATT_DIM), F32),
                   jax.ShapeDtypeStruct((t, ATT_DIM), BF16)],
        grid=(t // tm,),
        in_specs=[pl.BlockSpec((tm, D_MODEL), lambda i: (tile0 + i, 0)),
                  pl.BlockSpec((1, D_MODEL), lambda i: (0, 0)),
                  pl.BlockSpec((D_MODEL, 3 * ATT_DIM), lambda i: (0, 0)),
                  tab, tab, vec, vec],
        out_specs=[out, out, out, out, out],
        compiler_params=_params(("parallel",)),
        name="qkv_proj",
    )(x, gain, w_qkv, cos, sin, q_gain, k_gain)


def _lambda_full(lam_ref, lam_init):
    a = jnp.sum(lam_ref[0:1, :] * lam_ref[1:2, :], axis=-1, keepdims=True)
    b = jnp.sum(lam_ref[2:3, :] * lam_ref[3:4, :], axis=-1, keepdims=True)
    return jnp.exp(a) - jnp.exp(b) + lam_init


def _split_components(q):
    lane = lax.broadcasted_iota(jnp.int32, (1, LANES), 1)
    zero = jnp.zeros_like(q)
    return jnp.concatenate([jnp.where(lane < HEAD_DIM, q, zero), jnp.where(lane >= HEAD_DIM, q, zero)], axis=0)


def _diff_combine(acc, l, rows, lam, gain, lam_init):
    o1 = acc[:rows] / l[:rows]
    o2 = acc[rows:] / l[rows:]
    o = o1 - lam * o2
    ms = jnp.mean(o * o, axis=-1, keepdims=True)
    return o * lax.rsqrt(ms + EPS) * gain * (1.0 - lam_init)


def _attn_prompt_kernel(lam_ref, g_ref, q_ref, k_ref, v_ref, o_ref,
                        vext_scr, s_scr, p_scr, m_scr, alpha_scr, acc_scr, *, tq, tk, hps, lam_init):
    i = pl.program_id(2)
    rows = 2 * tq
    heads = range(hps)

    @pl.when(i == 0)
    def _():
        for hh in heads:
            vext_scr[hh, :, :LANES] = v_ref[:, hh * LANES:(hh + 1) * LANES]
            vext_scr[hh, :, LANES:] = jnp.ones((vext_scr.shape[1], LANES), BF16)

    qq = [_split_components(q_ref[:, hh * LANES:(hh + 1) * LANES]) for hh in heads]
    diag = (i * tq) // tk
    n_blocks = diag + 1

    def key_block(t):
        return jnp.where(t == 0, diag, t - 1)

    def scores(hh, blk):
        return _dot_nt(qq[hh], k_ref[pl.ds(pl.multiple_of(blk * tk, tk), tk), hh * LANES:(hh + 1) * LANES])

    def softmax(hh, slot, bias=None):
        s = s_scr[hh, slot]
        if bias is not None:
            s = s + bias
        m_old = m_scr[hh]
        m_new = jnp.maximum(m_old, jnp.max(s, axis=-1, keepdims=True))
        alpha_scr[hh, slot] = jnp.exp(m_old - m_new)
        for c in range(tk // LANES):
            sl = slice(c * LANES, (c + 1) * LANES)
            p_scr[hh, slot, :, sl] = jnp.exp(s[:, sl] - m_new).astype(BF16)
        m_scr[hh] = m_new

    def accumulate(hh, slot, blk):
        pv = _dot(p_scr[hh, slot], vext_scr[hh, pl.ds(pl.multiple_of(blk * tk, tk), tk), :])
        alpha = alpha_scr[hh, slot]
        acc_scr[hh, :, :LANES] = alpha * acc_scr[hh, :, :LANES] + pv[:, :LANES]
        acc_scr[hh, :, LANES:] = alpha * acc_scr[hh, :, LANES:] + pv[:, LANES:]

    m_scr[...] = jnp.full(m_scr.shape, -jnp.inf, F32)
    acc_scr[...] = jnp.zeros(acc_scr.shape, F32)
    q_chunk = (i * tq + lax.broadcasted_iota(jnp.int32, (rows, 1), 0) % tq) // CHUNK
    k_chunk = (diag * tk + lax.broadcasted_iota(jnp.int32, (1, tk), 1)) // CHUNK
    for hh in heads:
        p_scr[hh, 1] = jnp.zeros(p_scr.shape[2:], BF16)
        alpha_scr[hh, 1] = jnp.ones(alpha_scr.shape[2:], F32)
        s_scr[hh, 0] = jnp.where(k_chunk <= q_chunk, scores(hh, diag), -jnp.inf)

    last = n_blocks - 1

    def body(pair, carry):
        t = 2 * pair
        for hh in heads:
            s_scr[hh, 1] = scores(hh, key_block(jnp.minimum(t + 1, last)))
            softmax(hh, 0)
            accumulate(hh, 1, key_block(jnp.maximum(t - 1, 0)))
        for hh in heads:
            s_scr[hh, 0] = scores(hh, key_block(jnp.minimum(t + 2, last)))
            softmax(hh, 1, jnp.where(t + 1 <= last, 0.0, -jnp.inf))
            accumulate(hh, 0, key_block(t))
        return carry

    n_pairs = (n_blocks + 1) // 2
    lax.fori_loop(0, n_pairs, body, 0)
    lam = _lambda_full(lam_ref, lam_init)
    for hh in heads:
        accumulate(hh, 1, key_block(jnp.minimum(2 * n_pairs - 1, last)))
        acc = acc_scr[hh]
        o_ref[:, hh * LANES:(hh + 1) * LANES] = _diff_combine(
            acc[:, :LANES], acc[:, LANES:], tq, lam, g_ref[...], lam_init).astype(o_ref.dtype)


def attn_prompt(qb, kb, vb, lam_rows, sub_gain, lam_init, *, batch, seq, tq, tk, hps):
    nq = seq // tq
    assert tk % tq == 0 and tq % CHUNK == 0, "one key block must hold all chunks of a query block"
    assert N_HEADS % hps == 0
    kern = functools.partial(_attn_prompt_kernel, tq=tq, tk=tk, hps=hps, lam_init=lam_init)
    width = hps * LANES
    return pl.pallas_call(
        kern,
        out_shape=jax.ShapeDtypeStruct((batch * seq, ATT_DIM), BF16),
        grid=(batch, N_HEADS // hps, nq),
        in_specs=[pl.BlockSpec((4, HEAD_DIM), lambda b, h, i: (0, 0)),
                  pl.BlockSpec((1, LANES), lambda b, h, i: (0, 0)),
                  pl.BlockSpec((tq, width), lambda b, h, i: (b * nq + i, h)),
                  pl.BlockSpec((seq, width), lambda b, h, i: (b, h)),
                  pl.BlockSpec((seq, width), lambda b, h, i: (b, h))],
        out_specs=pl.BlockSpec((tq, width), lambda b, h, i: (b * nq + i, h)),
        scratch_shapes=[pltpu.VMEM((hps, seq, 2 * LANES), BF16),
                        pltpu.VMEM((hps, 2, 2 * tq, tk), F32), pltpu.VMEM((hps, 2, 2 * tq, tk), BF16),
                        pltpu.VMEM((hps, 2 * tq, LANES), F32), pltpu.VMEM((hps, 2, 2 * tq, LANES), F32),
                        pltpu.VMEM((hps, 2 * tq, 2 * LANES), F32)],
        compiler_params=_params(("parallel", "parallel", "arbitrary")),
        name="attn_prompt",
    )(lam_rows, sub_gain, qb, kb, vb)


def _attn_sample_kernel(lam_ref, g_ref, q_ref, kn_ref, vn_ref, kt_ref, vc_ref, o_ref,
                        qq_scr, m_scr, l_scr, acc_scr, *, rows, tk, lam_init):
    j = pl.program_id(1)
    hr = 2 * rows

    @pl.when(j == 0)
    def _():
        for h in range(N_HEADS):
            qq_scr[h * hr:(h + 1) * hr, :] = _split_components(q_ref[:, h * LANES:(h + 1) * LANES])
        m_scr[...] = jnp.full(m_scr.shape, -jnp.inf, F32)
        l_scr[...] = jnp.zeros(l_scr.shape, F32)
        acc_scr[...] = jnp.zeros(acc_scr.shape, F32)

    def update(scores_of, values_of):
        s = jnp.concatenate([scores_of(h, qq_scr[h * hr:(h + 1) * hr, :]) for h in range(N_HEADS)], axis=0)
        m_old = m_scr[...]
        m_new = jnp.maximum(m_old, jnp.max(s, axis=-1, keepdims=True))
        alpha = jnp.exp(m_old - m_new)
        p = jnp.exp(s - m_new)
        l_scr[...] = alpha * l_scr[...] + jnp.sum(p, axis=-1, keepdims=True)
        m_scr[...] = m_new
        p = p.astype(BF16)
        for h in range(N_HEADS):
            hs = slice(h * hr, (h + 1) * hr)
            acc_scr[hs, :] = alpha[hs] * acc_scr[hs, :] + _dot(p[hs], values_of(h))

    update(lambda h, qq: _dot(qq, kt_ref[h * LANES:(h + 1) * LANES, :].astype(BF16)),
           lambda h: vc_ref[pl.ds(h, tk, stride=N_HEADS), :].astype(BF16))

    @pl.when(j == pl.num_programs(1) - 1)
    def _():
        update(lambda h, qq: _dot_nt(qq, kn_ref[:, h * LANES:(h + 1) * LANES]),
               lambda h: vn_ref[:, h * LANES:(h + 1) * LANES])
        lam = _lambda_full(lam_ref, lam_init)
        for h in range(N_HEADS):
            hs = slice(h * hr, (h + 1) * hr)
            o_ref[:, h * LANES:(h + 1) * LANES] = _diff_combine(
                acc_scr[hs, :], l_scr[hs, :], rows, lam, g_ref[...], lam_init).astype(o_ref.dtype)


def attn_sample(qb, kb, vb, cache_kt, cache_v, lam_rows, sub_gain, lam_init, *, layer, batch, rows, tk):
    past = cache_kt.shape[3]
    new_blk = pl.BlockSpec((rows, ATT_DIM), lambda b, j: (b, 0))
    kern = functools.partial(_attn_sample_kernel, rows=rows, tk=tk, lam_init=lam_init)
    return pl.pallas_call(
        kern,
        out_shape=jax.ShapeDtypeStruct((batch * rows, ATT_DIM), BF16),
        grid=(batch, past // tk),
        in_specs=[pl.BlockSpec((4, HEAD_DIM), lambda b, j: (0, 0)),
                  pl.BlockSpec((1, LANES), lambda b, j: (0, 0)),
                  new_blk, new_blk, new_blk,
                  pl.BlockSpec((None, None, ATT_DIM, tk), lambda b, j: (layer, b, 0, j)),
                  pl.BlockSpec((None, None, tk * N_HEADS, 2 * HEAD_DIM), lambda b, j: (layer, b, j, 0))],
        out_specs=pl.BlockSpec((rows, ATT_DIM), lambda b, j: (b, 0)),
        scratch_shapes=[pltpu.VMEM((N_HEADS * 2 * rows, LANES), BF16),
                        pltpu.VMEM((N_HEADS * 2 * rows, 1), F32), pltpu.VMEM((N_HEADS * 2 * rows, 1), F32),
                        pltpu.VMEM((N_HEADS * 2 * rows, LANES), F32)],
        compiler_params=_params(("parallel", "arbitrary")),
        name="attn_sample",
    )(lam_rows, sub_gain, qb, kb, vb, cache_kt, cache_v)


def _ssd_kernel(xbc_ref, z_ref, dt_ref, conv0_ref, ssm0_ref, cw_ref, cb_ref, dtb_ref, alog_ref, dskip_ref, ng_ref,
                y_ref, ssm_ref, conv_ref, xpad_scr, *, lc):
    c = pl.program_id(1)
    pad0 = SUBLANES - (CONV_W - 1)

    @pl.when(c == 0)
    def _():
        xpad_scr[pad0:SUBLANES, :] = conv0_ref[...]
        ssm_ref[...] = ssm0_ref[...]

    xpad_scr[SUBLANES:, :] = xbc_ref[...]
    conv = cb_ref[...] + sum(xpad_scr[pad0 + j:pad0 + j + lc, :] * cw_ref[j:j + 1, :] for j in range(CONV_W))
    conv = _silu(conv)
    tail = xpad_scr[lc + pad0:lc + SUBLANES, :]
    xpad_scr[pad0:SUBLANES, :] = tail
    conv_ref[...] = tail

    dt = _softplus(dt_ref[...] + dtb_ref[...])
    da = dt * (-jnp.exp(alog_ref[...]))
    row = lax.broadcasted_iota(jnp.int32, (lc, lc), 0)
    col = lax.broadcasted_iota(jnp.int32, (lc, lc), 1)
    causal = row >= col
    a_cum = _dot_exact(causal.astype(F32), da)
    dt_t = _transpose_exact(dt)
    a_cum_t = _transpose_exact(a_cum)
    eye = (lax.broadcasted_iota(jnp.int32, (LANES, LANES), 0)
           == lax.broadcasted_iota(jnp.int32, (LANES, LANES), 1)).astype(BF16)

    for g in range(SSM_GROUPS):
        b_g = conv[:, D_INNER + g * D_STATE:D_INNER + (g + 1) * D_STATE].astype(BF16)
        c_off = D_INNER + SSM_GROUPS * D_STATE
        c_g = conv[:, c_off + g * D_STATE:c_off + (g + 1) * D_STATE].astype(BF16)
        cb = _dot_nt(c_g, b_g)
        ys = []
        for pair in range(HEADS_PER_GROUP // 2):
            lane0 = (g * HEADS_PER_GROUP + 2 * pair) * SSM_HEAD_DIM
            x_pair = conv[:, lane0:lane0 + LANES].astype(BF16)
            x_pair_t = _dot_nt(eye, x_pair)
            for sub in range(2):
                h = g * HEADS_PER_GROUP + 2 * pair + sub
                x_h = x_pair[:, sub * SSM_HEAD_DIM:(sub + 1) * SSM_HEAD_DIM]
                x_h_t = x_pair_t[sub * SSM_HEAD_DIM:(sub + 1) * SSM_HEAD_DIM, :]
                a_col = a_cum[:, h:h + 1]
                a_row = a_cum_t[h:h + 1, :]
                dt_row = dt_t[h:h + 1, :]
                a_last = a_cum_t[h:h + 1, lc - 1:lc]
                decay = jnp.exp(jnp.where(causal, a_col - a_row, -jnp.inf))
                w_in = (cb * decay * dt_row).astype(BF16)
                state = ssm_ref[h]
                y_h = _dot(w_in, x_h) + jnp.exp(a_col) * _dot_nt(c_g, state.astype(BF16))
                ys.append(y_h)
                w_state = (x_h_t * (dt_row * jnp.exp(a_last - a_row))).astype(BF16)
                ssm_ref[h] = jnp.exp(a_last) * state + _dot(w_state, b_g)
        lanes = slice(g * HEADS_PER_GROUP * SSM_HEAD_DIM, (g + 1) * HEADS_PER_GROUP * SSM_HEAD_DIM)
        y_g = jnp.concatenate(ys, axis=-1) + dskip_ref[:, lanes] * conv[:, lanes]
        y_g = y_g * _silu(z_ref[:, lanes])
        ms = jnp.mean(y_g * y_g, axis=-1, keepdims=True)
        y_ref[:, lanes] = (y_g * lax.rsqrt(ms + EPS) * ng_ref[:, lanes]).astype(y_ref.dtype)


def ssd(proj, conv0, ssm0, conv_w, conv_b, dt_bias, a_log, d_skip, norm_gain, *, batch, seq, row0, lc):
    nc = seq // lc
    tok = lambda width, col: pl.BlockSpec((lc, width), lambda b, c: (row0 // lc + b * nc + c, col // width))
    const = lambda r, w: pl.BlockSpec((r, w), lambda b, c: (0, 0))
    kern = functools.partial(_ssd_kernel, lc=lc)
    return pl.pallas_call(
        kern,
        out_shape=[jax.ShapeDtypeStruct((batch * seq, D_INNER), BF16),
                   jax.ShapeDtypeStruct((batch, SSM_HEADS, SSM_HEAD_DIM, D_STATE), F32),
                   jax.ShapeDtypeStruct((batch, CONV_W - 1, CONV_DIM), F32)],
        grid=(batch, nc),
        in_specs=[tok(CONV_DIM, COL_XBC), tok(D_INNER, COL_Z), tok(LANES, COL_DT),
                  pl.BlockSpec((None, CONV_W - 1, CONV_DIM), lambda b, c: (b, 0, 0)),
                  pl.BlockSpec((None, SSM_HEADS, SSM_HEAD_DIM, D_STATE), lambda b, c: (b, 0, 0, 0)),
                  const(CONV_W, CONV_DIM), const(1, CONV_DIM), const(1, LANES), const(1, LANES),
                  const(1, D_INNER), const(1, D_INNER)],
        out_specs=[pl.BlockSpec((lc, D_INNER), lambda b, c: (b * nc + c, 0)),
                   pl.BlockSpec((None, SSM_HEADS, SSM_HEAD_DIM, D_STATE), lambda b, c: (b, 0, 0, 0)),
                   pl.BlockSpec((None, CONV_W - 1, CONV_DIM), lambda b, c: (b, 0, 0))],
        scratch_shapes=[pltpu.VMEM((SUBLANES + lc, CONV_DIM), F32)],
        compiler_params=_params(("parallel", "arbitrary")),
        name="ssd",
    )(proj, proj, proj, conv0, ssm0, conv_w, conv_b, dt_bias, a_log, d_skip, norm_gain)


def _out_proj_kernel(x_ref, attp_ref, atts_ref, yp_ref, ys_ref, ga_ref, gm_ref, wa_ref, wm_ref, wo_ref, o_ref, *,
                     prompt_tiles):
    is_prompt = pl.program_id(0) < prompt_tiles
    att = jnp.where(is_prompt, attp_ref[...], atts_ref[...])
    y = jnp.where(is_prompt, yp_ref[...], ys_ref[...])
    merged = (_sigmoid(ga_ref[...]) * _dot(att, wa_ref[...]) + _sigmoid(gm_ref[...]) * _dot(y, wm_ref[...]))
    o_ref[...] = x_ref[...] + _dot(merged.astype(BF16), wo_ref[...])


def out_proj(x, att_p, att_s, y_p, y_s, proj, wa, wm, wo, *, tm):
    t = x.shape[0]
    n_p, n_s = att_p.shape[0] // tm, att_s.shape[0] // tm
    assert n_p * tm == att_p.shape[0] and n_s * tm == att_s.shape[0] and n_p + n_s == t // tm
    gate = lambda c: pl.BlockSpec((tm, D_MODEL), lambda i, c=c: (i, c // D_MODEL))
    const = lambda r: pl.BlockSpec((r, D_MODEL), lambda i: (0, 0))
    prompt = lambda w: pl.BlockSpec((tm, w), lambda i: (jnp.minimum(i, n_p - 1), 0))
    sample = lambda w: pl.BlockSpec((tm, w), lambda i: (jnp.maximum(i - n_p, 0), 0))
    return pl.pallas_call(
        functools.partial(_out_proj_kernel, prompt_tiles=n_p),
        out_shape=jax.ShapeDtypeStruct((t, D_MODEL), F32),
        grid=(t // tm,),
        in_specs=[pl.BlockSpec((tm, D_MODEL), lambda i: (i, 0)),
                  prompt(ATT_DIM), sample(ATT_DIM), prompt(D_INNER), sample(D_INNER),
                  gate(COL_GATE), gate(COL_GATE + D_MODEL),
                  const(ATT_DIM), const(D_INNER), const(D_MODEL)],
        out_specs=pl.BlockSpec((tm, D_MODEL), lambda i: (i, 0)),
        compiler_params=_params(("parallel",)),
        name="out_proj",
    )(x, att_p, att_s, y_p, y_s, proj, proj, wa, wm, wo)


def _top_values(s, k):
    vals = []
    for _ in range(k):
        m = jnp.max(s, axis=0, keepdims=True)
        vals.append(m)
        s = jnp.where(s == m, -jnp.inf, s)
    return vals


def _top_ranked(s, k):
    vals = []
    rank = jnp.full(s.shape, float(k), F32)
    for i in range(k):
        m = jnp.max(s, axis=0, keepdims=True)
        hit = s == m
        vals.append(m)
        rank = jnp.where(hit, float(i), rank)
        s = jnp.where(hit, -jnp.inf, s)
    return vals, rank


def _peer_keys_kernel(x_ref, g_ref, wqt_ref, k1_ref, k2_ref, ht_ref, e1_ref, cnt_ref, e2_ref, r2_ref):
    x = x_ref[...]
    ms = jnp.mean(x * x, axis=-1, keepdims=True)
    h32 = x * lax.rsqrt(ms + EPS) * g_ref[...]
    h_t = h32.T.astype(BF16)
    ht_ref[...] = h_t
    q_t = _dot(wqt_ref[...], h_t)
    for hd in range(PK_HEADS):
        r0 = hd * 2 * PK_HALF
        s1 = _dot(k1_ref[hd], q_t[r0:r0 + PK_HALF, :].astype(BF16))
        s2 = _dot(k2_ref[hd], q_t[r0 + PK_HALF:r0 + 2 * PK_HALF, :].astype(BF16))
        v1 = _top_values(s1, PK_TOPK)
        v2, r2 = _top_ranked(s2, PK_TOPK)
        widths = [PK_TOPK // (a + 1) for a in range(PK_TOPK)]
        pad_rows = -sum(widths) % SUBLANES
        cand = jnp.concatenate([v1[a] + jnp.concatenate(v2[:widths[a]], axis=0) for a in range(PK_TOPK)]
                               + [jnp.full((pad_rows, s1.shape[1]), -jnp.inf, F32)], axis=0)
        tau = _top_values(cand, PK_TOPK)[-1]
        top = v1[0] + v2[0]
        chosen = cand >= tau
        z = jnp.sum(jnp.where(chosen, jnp.exp(cand - top), 0.0), axis=0, keepdims=True)
        cnt = jnp.zeros(s1.shape, F32)
        row0 = 0
        for a in range(PK_TOPK):
            n_a = jnp.sum(chosen[row0:row0 + widths[a]].astype(F32), axis=0, keepdims=True)
            cnt = jnp.where(s1 == v1[a], n_a, cnt)
            row0 += widths[a]
        e1_ref[hd] = jnp.exp(s1 - v1[0]) / z
        cnt_ref[hd] = cnt
        e2_ref[hd] = jnp.exp(s2 - v2[0])
        r2_ref[hd] = r2


def peer_keys(x, gain, wq_t, k1, k2, *, tt):
    t = x.shape[0]
    per_head = pl.BlockSpec((PK_HEADS, N_KEYS, tt), lambda i: (0, 0, i))
    table = jax.ShapeDtypeStruct((PK_HEADS, N_KEYS, t), F32)
    return pl.pallas_call(
        _peer_keys_kernel,
        out_shape=[jax.ShapeDtypeStruct((D_MODEL, t), BF16), table, table, table, table],
        grid=(t // tt,),
        in_specs=[pl.BlockSpec((tt, D_MODEL), lambda i: (i, 0)),
                  pl.BlockSpec((1, D_MODEL), lambda i: (0, 0)),
                  pl.BlockSpec((PK_HEADS * 2 * PK_HALF, D_MODEL), lambda i: (0, 0)),
                  pl.BlockSpec((PK_HEADS, N_KEYS, PK_HALF), lambda i: (0, 0, 0)),
                  pl.BlockSpec((PK_HEADS, N_KEYS, PK_HALF), lambda i: (0, 0, 0))],
        out_specs=[pl.BlockSpec((D_MODEL, tt), lambda i: (0, i)), per_head, per_head, per_head, per_head],
        compiler_params=_params(("parallel",)),
        name="peer_keys",
    )(x, gain, wq_t, k1, k2)


def _gelu(x):
    return 0.5 * x * (1.0 + lax.erf(x * (2.0 ** -0.5)))


def _peer_mix_kernel(*refs, block, eb, first, last):
    refs = list(refs)
    h_ref, u_ref, vt_ref, e1_ref, cnt_ref, e2_ref, r2_ref = refs[:7]
    del refs[:7]
    accin_ref = None if first else refs.pop(0)
    x_ref = refs.pop(0) if last else None
    acc_ref = refs.pop(0)
    o_ref = refs.pop(0) if last else None
    raw_scr, coef_scr, e2_scr, r2_scr = refs
    tt = acc_ref.shape[1]
    n_sub = eb // PEER_SUB
    rows_per_sub = PEER_SUB // N_KEYS
    pack = 2 * SUBLANES

    acc_ref[...] = jnp.zeros(acc_ref.shape, F32) if first else accin_ref[...]
    for hd in range(PK_HEADS):
        e2_scr[hd] = e2_ref[hd].astype(BF16)
        r2_scr[hd] = r2_ref[hd].astype(BF16)

    def pre_activations(j):
        start = pl.multiple_of(j * PEER_SUB, PEER_SUB)
        return _dot(u_ref[pl.ds(start, PEER_SUB), :], h_ref[...])

    def mix(slot, j):
        zero = jnp.zeros((pack, LANES), BF16)
        for r in range(rows_per_sub):
            i1 = (block * n_sub + j) * rows_per_sub + r
            cnt_rows = [cnt_ref[hd, pl.ds(i1, 1), :] for hd in range(PK_HEADS)]
            e1_rows = [e1_ref[hd, pl.ds(i1, 1), :] for hd in range(PK_HEADS)]
            for lt in range(tt // LANES):
                lanes = slice(lt * LANES, (lt + 1) * LANES)
                cnt_b = [jnp.broadcast_to(row[:, lanes], (pack, LANES)).astype(BF16) for row in cnt_rows]
                e1_b = [jnp.broadcast_to(row[:, lanes], (pack, LANES)).astype(BF16) for row in e1_rows]
                for c in range(N_KEYS // pack):
                    keys = slice(c * pack, (c + 1) * pack)
                    gate = zero
                    for hd in range(PK_HEADS):
                        keep = r2_scr[hd, keys, lanes] < cnt_b[hd]
                        gate = gate + jnp.where(keep, e2_scr[hd, keys, lanes] * e1_b[hd], zero)
                    rows = slice(r * N_KEYS + c * pack, r * N_KEYS + (c + 1) * pack)
                    coef_scr[slot, rows, lanes] = gate * _gelu(raw_scr[slot, rows, lanes]).astype(BF16)

    def accumulate(slot, j):
        acc_ref[...] += _dot(vt_ref[j], coef_scr[slot])

    raw_scr[0] = pre_activations(0)
    coef_scr[1] = jnp.zeros(coef_scr.shape[1:], BF16)

    def body(pair, carry):
        j = 2 * pair
        raw_scr[1] = pre_activations(j + 1)
        mix(0, j)
        accumulate(1, jnp.maximum(j - 1, 0))
        raw_scr[0] = pre_activations(jnp.minimum(j + 2, n_sub - 1))
        mix(1, j + 1)
        accumulate(0, j)
        return carry

    lax.fori_loop(0, n_sub // 2, body, 0)
    accumulate(1, n_sub - 1)
    if last:
        o_ref[...] = x_ref[...] + acc_ref[...].T


def peer_mix(x, h_t, u, v_sub, e1, cnt, e2, r2, *, tt, eb):
    t = x.shape[0]
    n_blocks = N_EXPERTS // eb
    assert (eb // PEER_SUB) % 2 == 0
    per_head = pl.BlockSpec((PK_HEADS, N_KEYS, tt), lambda i: (0, 0, i))
    acc_blk = pl.BlockSpec((D_MODEL, tt), lambda i: (0, i))
    tok_blk = pl.BlockSpec((tt, D_MODEL), lambda i: (i, 0))
    acc_shape = jax.ShapeDtypeStruct((D_MODEL, t), F32)
    acc = None
    for block in range(n_blocks):
        first, last = block == 0, block == n_blocks - 1
        args = [h_t, u, v_sub, e1, cnt, e2, r2] + ([] if first else [acc]) + ([x] if last else [])
        in_specs = ([acc_blk,
                     pl.BlockSpec((eb, D_MODEL), lambda i, block=block: (block, 0), pipeline_mode=pl.Buffered(1)),
                     pl.BlockSpec((eb // PEER_SUB, D_MODEL, PEER_SUB), lambda i, block=block: (block, 0, 0),
                                  pipeline_mode=pl.Buffered(1)),
                     per_head, per_head, per_head, per_head]
                    + ([] if first else [acc_blk]) + ([tok_blk] if last else []))
        result = pl.pallas_call(
            functools.partial(_peer_mix_kernel, block=block, eb=eb, first=first, last=last),
            out_shape=[acc_shape] + ([jax.ShapeDtypeStruct((t, D_MODEL), F32)] if last else []),
            grid=(t // tt,),
            in_specs=in_specs,
            out_specs=[acc_blk] + ([tok_blk] if last else []),
            scratch_shapes=[pltpu.VMEM((2, PEER_SUB, tt), F32), pltpu.VMEM((2, PEER_SUB, tt), BF16),
                            pltpu.VMEM((PK_HEADS, N_KEYS, tt), BF16), pltpu.VMEM((PK_HEADS, N_KEYS, tt), BF16)],
            input_output_aliases={} if first else {7: 0},
            compiler_params=_params(("parallel",)),
            name="peer_mix",
        )(*args)
        acc = result[0]
    return result[1]


def _rope_tables(pos):
    half = HEAD_DIM // 2
    inv_freq = ROPE_THETA ** (-jnp.arange(half, dtype=F32) / half)
    ang = pos[:, None] * inv_freq[None, :]
    cos, sin = jnp.cos(ang), jnp.sin(ang)
    reps = LANES // HEAD_DIM
    return jnp.tile(jnp.concatenate([cos, cos], -1), (1, reps)), jnp.tile(jnp.concatenate([-sin, sin], -1), (1, reps))


def _lambda_init(layer):
    return 0.8 - 0.6 * math.exp(-0.3 * layer)


def _pad_lanes(v, width=LANES):
    return jnp.pad(v, (0, width - v.shape[0])).reshape(1, width)


def kernel(x_prompt, x_sample, cache_k, cache_v, state_ssm, state_conv, norm_mix, w_in, q_norm, k_norm, lam_q1, lam_k1, lam_q2, lam_k2, attn_subln, conv_w, conv_b, dt_bias, a_log, d_skip, ssm_norm, w_proj_attn, w_proj_ssm, w_out, norm_ffn, peer_wq, peer_k1, peer_k2, peer_u, peer_v):
    bp, lp, _ = x_prompt.shape
    bs, ls, _ = x_sample.shape
    depth = w_in.shape[0]
    n_past = cache_k.shape[2]
    tp, ts = bp * lp, bs * ls
    x = jnp.concatenate([x_prompt.reshape(tp, D_MODEL), x_sample.reshape(ts, D_MODEL)], axis=0)
    t_all = tp + ts
    tm = math.gcd(t_all, 1024)
    tt = math.gcd(t_all, 256)

    cos_p, sin_p = _rope_tables(jnp.tile(jnp.arange(lp, dtype=F32), bp))
    cos_s, sin_s = _rope_tables(jnp.tile(n_past + jnp.arange(ls, dtype=F32), bs))
    cache_kt = jnp.transpose(cache_k, (0, 1, 3, 4, 5, 2)).reshape(depth, bs, ATT_DIM, n_past)
    cache_v = cache_v.reshape(depth, bs, n_past * N_HEADS, 2 * HEAD_DIM)
    conv0_p = jnp.zeros((bp, CONV_W - 1, CONV_DIM), F32)
    ssm0_p = jnp.zeros((bp, SSM_HEADS, SSM_HEAD_DIM, D_STATE), F32)

    outs = {name: [] for name in ("kp", "vp", "sp", "cp", "ks", "vs", "ss", "cs")}
    for i in range(depth):
        li = _lambda_init(i)
        w_i = w_in[i].astype(BF16)
        w_qkv = w_i[:, :3 * ATT_DIM]
        w_rest = jnp.zeros((D_MODEL, PROJ_COLS), BF16)
        for col, piece in ((COL_Z, w_i[:, 3072:5120]), (COL_GATE, w_i[:, 9248:]), (COL_XBC, w_i[:, 5120:9216]),
                           (COL_DT, w_i[:, 9216:9248])):
            w_rest = lax.dynamic_update_slice(w_rest, piece, (0, col))
        gain_mix = norm_mix[i].reshape(1, D_MODEL)
        proj = in_proj(x, gain_mix, w_rest, tm=tm)

        q_gain = jnp.tile(q_norm[i], LANES // HEAD_DIM).reshape(1, LANES)
        k_gain = jnp.tile(k_norm[i], LANES // HEAD_DIM).reshape(1, LANES)
        qkv_w = (gain_mix, w_qkv)
        qb_p, k_p, kb_p, v_p, vb_p = qkv_proj(x, *qkv_w, cos_p, sin_p, q_gain, k_gain, row0=0, rows=tp,
                                              tm=math.gcd(tp, 512))
        qb_s, k_s, kb_s, v_s, vb_s = qkv_proj(x, *qkv_w, cos_s, sin_s, q_gain, k_gain, row0=tp, rows=ts,
                                              tm=math.gcd(math.gcd(tp, ts), 512))

        lam_rows = jnp.stack([lam_q1[i], lam_k1[i], lam_q2[i], lam_k2[i]])
        sub_gain = attn_subln[i].reshape(1, LANES)
        att_p = attn_prompt(qb_p, kb_p, vb_p, lam_rows, sub_gain, li, batch=bp, seq=lp,
                            tq=min(ATT_TQ, lp), tk=min(ATT_TK, lp), hps=ATT_HPS)
        att_s = attn_sample(qb_s, kb_s, vb_s, cache_kt, cache_v, lam_rows, sub_gain, li,
                            layer=i, batch=bs, rows=ls, tk=min(ATT_TKS, n_past))

        ssd_w = (conv_w[i], conv_b[i].reshape(1, CONV_DIM), _pad_lanes(dt_bias[i]), _pad_lanes(a_log[i]),
                 jnp.repeat(d_skip[i], SSM_HEAD_DIM).reshape(1, D_INNER), ssm_norm[i].reshape(1, D_INNER))
        y_p, ssm_p, conv_p = ssd(proj, conv0_p, ssm0_p, *ssd_w, batch=bp, seq=lp, row0=0, lc=min(SSD_CHUNK, lp))
        y_s, ssm_s, conv_s = ssd(proj, state_conv[i], state_ssm[i], *ssd_w, batch=bs, seq=ls, row0=tp,
                                 lc=min(SSD_CHUNK, ls))

        x = out_proj(x, att_p, att_s, y_p, y_s, proj, w_proj_attn[i].astype(BF16), w_proj_ssm[i].astype(BF16),
                     w_out[i].astype(BF16), tm=math.gcd(math.gcd(tp, ts), 512))

        h2, *tables = peer_keys(x, norm_ffn[i].reshape(1, D_MODEL), peer_wq[i].T.astype(BF16),
                                peer_k1[i].astype(BF16), peer_k2[i].astype(BF16), tt=tt)
        v_sub = peer_v[i].astype(BF16).reshape(N_EXPERTS // PEER_SUB, PEER_SUB, D_MODEL).transpose(0, 2, 1)
        x = peer_mix(x, h2, peer_u[i].astype(BF16), v_sub, *tables, tt=tt, eb=PEER_EB)

        outs["kp"].append(k_p.reshape(bp, lp, N_HEADS, 2, HEAD_DIM))
        outs["vp"].append(v_p.reshape(bp, lp, N_HEADS, 2 * HEAD_DIM))
        outs["sp"].append(ssm_p)
        outs["cp"].append(conv_p)
        outs["ks"].append(k_s.reshape(bs, ls, N_HEADS, 2, HEAD_DIM))
        outs["vs"].append(v_s.reshape(bs, ls, N_HEADS, 2 * HEAD_DIM))
        outs["ss"].append(ssm_s)
        outs["cs"].append(conv_s)

    stack = lambda name: jnp.stack(outs[name])
    return (x[:tp].reshape(bp, lp, D_MODEL), x[tp:].reshape(bs, ls, D_MODEL),
            stack("kp"), stack("vp"), stack("sp"), stack("cp"),
            stack("ks"), stack("vs"), stack("ss"), stack("cs"))
```
